```python
import math
import jax, jax.numpy as jnp
from jax import lax
import numpy as np

D_MODEL = 2048
BATCH = 2
SEQ = 4096
DEPTH = 1
DEC_BATCH = 8
DEC_SEQ = 1
PAST_LEN = 16384
PAGE_SIZE = 128

MIX_WIDTH = D_MODEL
ATTN_WIDTH = MIX_WIDTH // 2
N_ATTN_HEADS = 8
DV = ATTN_WIDTH // N_ATTN_HEADS
DK = DV // 2
D_INNER = MIX_WIDTH - ATTN_WIDTH
SSM_HEADDIM = 64
N_SSM_HEADS = D_INNER // SSM_HEADDIM
N_GROUPS = 2
HEADS_PER_GROUP = N_SSM_HEADS // N_GROUPS
D_STATE = 128
CONV_W = 4
CONV_DIM = D_INNER + 2 * N_GROUPS * D_STATE
CHUNK = 128
Q_BLOCK = 128
D_FF = 4 * D_MODEL
PLE_DIM = 256
D_PROJ = 3 * ATTN_WIDTH + D_INNER + CONV_DIM + N_SSM_HEADS
NORM_EPS = 1e-6
SUBLN_EPS = 1e-5

kernel_name = "hymba_diffattn_ssd_step"


def _rmsnorm(x, g, eps=NORM_EPS):
    xf = x.astype(jnp.float32)
    y = xf * lax.rsqrt(jnp.mean(xf * xf, axis=-1, keepdims=True) + eps)
    return (y * g.astype(jnp.float32)).astype(x.dtype)


def _alibi_slopes():
    return 2.0 ** (-8.0 * jnp.arange(1, N_ATTN_HEADS + 1, dtype=jnp.float32) / N_ATTN_HEADS)


def _split_proj(u, w_in):
    proj = u @ w_in
    cuts = np.cumsum([ATTN_WIDTH, ATTN_WIDTH, ATTN_WIDTH, D_INNER, CONV_DIM]).tolist()
    q, k, v, z, xbc, dt = jnp.split(proj, cuts, axis=-1)
    lead = u.shape[:-1]
    q = q.reshape(*lead, N_ATTN_HEADS, 2, DK)
    k = k.reshape(*lead, N_ATTN_HEADS, 2, DK)
    v = v.reshape(*lead, N_ATTN_HEADS, DV)
    return q, k, v, z, xbc, dt


def _diff_weights(s, lam):
    p = jax.nn.softmax(s, axis=-1)
    return p[:, :, 0] - lam * p[:, :, 1]


def _prompt_diff_attn(q, k, v, lam):
    b, s = q.shape[:2]
    nb = s // Q_BLOCK
    scale = DK ** -0.5
    slopes = _alibi_slopes()
    kpos = jnp.arange(s)
    qb = jnp.moveaxis(q.reshape(b, nb, Q_BLOCK, N_ATTN_HEADS, 2, DK), 1, 0)
    starts = jnp.arange(nb) * Q_BLOCK

    def one_block(args):
        qblk, st = args
        sc = jnp.einsum('bqhcd,bkhcd->bhcqk', qblk, k).astype(jnp.float32) * scale
        dist = ((st + jnp.arange(Q_BLOCK))[:, None] - kpos[None, :]).astype(jnp.float32)
        bias = jnp.where(dist[None] >= 0, -slopes[:, None, None] * dist[None], -jnp.inf)
        w = _diff_weights(sc + bias[None, :, None], lam)
        return jnp.einsum('bhqk,bkhd->bqhd', w.astype(v.dtype), v)

    o = lax.map(one_block, (qb, starts))
    return jnp.moveaxis(o, 0, 1).reshape(b, s, N_ATTN_HEADS, DV)


def _sample_diff_attn(q, k_new, v_new, k_past, v_past, lam):
    t = q.shape[1]
    p = k_past.shape[1]
    scale = DK ** -0.5
    slopes = _alibi_slopes()
    dist_p = ((p + jnp.arange(t))[:, None] - jnp.arange(p)[None, :]).astype(jnp.float32)
    bias_p = -slopes[:, None, None] * dist_p[None]
    dist_n = (jnp.arange(t)[:, None] - jnp.arange(t)[None, :]).astype(jnp.float32)
    bias_n = jnp.where(dist_n[None] >= 0, -slopes[:, None, None] * dist_n[None], -jnp.inf)
    s_p = jnp.einsum('bqhcd,bkhcd->bhcqk', q, k_past).astype(jnp.float32) * scale + bias_p[None, :, None]
    s_n = jnp.einsum('bqhcd,bkhcd->bhcqk', q, k_new).astype(jnp.float32) * scale + bias_n[None, :, None]
    w = _diff_weights(jnp.concatenate([s_p, s_n], axis=-1), lam).astype(v_new.dtype)
    return (jnp.einsum('bhqk,bkhd->bqhd', w[..., :p], v_past)
            + jnp.einsum('bhqk,bkhd->bqhd', w[..., p:], v_new))


def _attn_out(o, w_sub, lam_init):
    o = _rmsnorm(o, w_sub, SUBLN_EPS) * (1.0 - lam_init)
    return o.reshape(*o.shape[:-2], ATTN_WIDTH)


def _causal_conv(xpad, w, b):
    t = xpad.shape[1] - (CONV_W - 1)
    out = b + sum(xpad[:, j:j + t] * w[j] for j in range(CONV_W))
    return jax.nn.silu(out)


def _ssm_inputs(xbc_act, dt_raw, dt_bias, a_log):
    lead = xbc_act.shape[:-1]
    xs, bm, cm = jnp.split(xbc_act, [D_INNER, D_INNER + N_GROUPS * D_STATE], axis=-1)
    xs = xs.reshape(*lead, N_SSM_HEADS, SSM_HEADDIM).astype(jnp.float32)
    bm = bm.reshape(*lead, N_GROUPS, D_STATE).astype(jnp.float32)
    cm = cm.reshape(*lead, N_GROUPS, D_STATE).astype(jnp.float32)
    dt = jax.nn.softplus(dt_raw.astype(jnp.float32) + dt_bias.astype(jnp.float32))
    a = -jnp.exp(a_log.astype(jnp.float32))
    return xs, bm, cm, dt, a


def _ssd_chunked(xs, dt, a, bm, cm):
    b, s = xs.shape[:2]
    c = s // CHUNK
    xd = (xs * dt[..., None]).reshape(b, c, CHUNK, N_GROUPS, HEADS_PER_GROUP, SSM_HEADDIM)
    da = (dt * a).reshape(b, c, CHUNK, N_GROUPS, HEADS_PER_GROUP).transpose(0, 3, 4, 1, 2)
    a_cs = jnp.cumsum(da, axis=-1)
    bc = bm.reshape(b, c, CHUNK, N_GROUPS, D_STATE)
    cc = cm.reshape(b, c, CHUNK, N_GROUPS, D_STATE)
    causal = jnp.tril(jnp.ones((CHUNK, CHUNK), dtype=bool))
    seg = a_cs[..., :, None] - a_cs[..., None, :]
    lmat = jnp.exp(jnp.where(causal, seg, -jnp.inf))
    cb = jnp.einsum('bclgn,bcsgn->bcgls', cc, bc)
    y_diag = jnp.einsum('bcgls,bgjcls,bcsgjp->bclgjp', cb, lmat, xd)
    decay_states = jnp.exp(a_cs[..., -1:] - a_cs)
    states = jnp.einsum('bcsgn,bgjcs,bcsgjp->cbgjpn', bc, decay_states, xd)
    chunk_decay = jnp.exp(a_cs[..., -1]).transpose(3, 0, 1, 2)

    def step(carry, inp):
        st, dec = inp
        return carry * dec[..., None, None] + st, carry

    init = jnp.zeros((b, N_GROUPS, HEADS_PER_GROUP, SSM_HEADDIM, D_STATE), jnp.float32)
    final, states_in = lax.scan(step, init, (states, chunk_decay))
    y_off = jnp.einsum('bclgn,cbgjpn,bgjcl->bclgjp', cc, states_in, jnp.exp(a_cs))
    y = (y_diag + y_off).reshape(b, s, N_SSM_HEADS, SSM_HEADDIM)
    return y, final.reshape(b, N_SSM_HEADS, SSM_HEADDIM, D_STATE)


def _ssm_recurrent(xs, dt, a, bm, cm, state0):
    db, t = xs.shape[:2]
    xt = jnp.moveaxis(xs.reshape(db, t, N_GROUPS, HEADS_PER_GROUP, SSM_HEADDIM), 1, 0)
    dtt = jnp.moveaxis(dt.reshape(db, t, N_GROUPS, HEADS_PER_GROUP), 1, 0)
    bt = jnp.moveaxis(bm, 1, 0)
    ct = jnp.moveaxis(cm, 1, 0)
    a_gj = a.reshape(N_GROUPS, HEADS_PER_GROUP)

    def step(st, inp):
        x_, d_, b_, c_ = inp
        st = st * jnp.exp(d_ * a_gj)[..., None, None] + jnp.einsum('bgjp,bgn->bgjpn', x_ * d_[..., None], b_)
        return st, jnp.einsum('bgjpn,bgn->bgjp', st, c_)

    st0 = state0.astype(jnp.float32).reshape(db, N_GROUPS, HEADS_PER_GROUP, SSM_HEADDIM, D_STATE)
    final, ys = lax.scan(step, st0, (xt, dtt, bt, ct))
    y = jnp.moveaxis(ys, 0, 1).reshape(db, t, N_SSM_HEADS, SSM_HEADDIM)
    return y, final.reshape(db, N_SSM_HEADS, SSM_HEADDIM, D_STATE)


def _ssm_out(y, xs, d_skip, z, w_norm):
    lead = z.shape[:-1]
    y = y + d_skip.astype(jnp.float32)[:, None] * xs
    g = (y.reshape(*lead, D_INNER) * jax.nn.silu(z.astype(jnp.float32)))
    g = g.reshape(*lead, N_GROUPS, D_INNER // N_GROUPS)
    g = g * lax.rsqrt(jnp.mean(g * g, axis=-1, keepdims=True) + NORM_EPS)
    return (g.reshape(*lead, D_INNER) * w_norm.astype(jnp.float32)).astype(z.dtype)


def _channel_and_ple(h, mix, p_l, w_out, g_ffn, w_up, w_down, w_ple, w_ple_norm, w_ple_gate):
    h = h + mix @ w_out
    u = _rmsnorm(h, g_ffn)
    h = h + jnp.square(jax.nn.relu(u @ w_up)) @ w_down
    e = _rmsnorm(p_l @ w_ple, w_ple_norm) * jax.nn.sigmoid(h @ w_ple_gate)
    return h + e


def setup_inputs(seed: int = 0) -> dict:
    key = jax.random.key(seed)
    kit = iter(jax.random.split(key, 40))

    def nrm(shape, scale):
        return jax.random.normal(next(kit), shape, jnp.float32) * scale

    n_pages = PAST_LEN // PAGE_SIZE
    n_used = DEC_BATCH * n_pages
    n_pool = n_used + max(1, n_used // 4)
    perm = jax.random.permutation(next(kit), n_pool)
    page_table = perm[:n_used].reshape(DEC_BATCH, n_pages).astype(jnp.int32)
    dt0 = jnp.exp(jax.random.uniform(next(kit), (DEPTH, N_SSM_HEADS), jnp.float32,
                                     minval=math.log(1e-3), maxval=math.log(1e-1)))
    dt_bias = dt0 + jnp.log(-jnp.expm1(-dt0))
    a_log = jnp.log(jax.random.uniform(next(kit), (DEPTH, N_SSM_HEADS), jnp.float32, minval=1.0, maxval=16.0))
    return {
        "x_prompt": nrm((BATCH, SEQ, D_MODEL), 1.0),
        "x_sample": nrm((DEC_BATCH, DEC_SEQ, D_MODEL), 1.0),
        "p_prompt": nrm((DEPTH, BATCH, SEQ, PLE_DIM), 1.0),
        "p_sample": nrm((DEPTH, DEC_BATCH, DEC_SEQ, PLE_DIM), 1.0),
        "cache_k": nrm((DEPTH, n_pool, PAGE_SIZE, N_ATTN_HEADS, 2 * DK), 1.0),
        "cache_v": nrm((DEPTH, n_pool, PAGE_SIZE, N_ATTN_HEADS, DV), 1.0),
        "page_table": page_table,
        "state_ssm": nrm((DEPTH, DEC_BATCH, N_SSM_HEADS, SSM_HEADDIM, D_STATE), 0.5),
        "state_conv": nrm((DEPTH, DEC_BATCH, CONV_W - 1, CONV_DIM), 1.0),
        "g_mix": 1.0 + nrm((DEPTH, D_MODEL), 0.02),
        "w_in": nrm((DEPTH, D_MODEL, D_PROJ), D_MODEL ** -0.5),
        "lambda_q1": nrm((DEPTH, DK), 0.1),
        "lambda_k1": nrm((DEPTH, DK), 0.1),
        "lambda_q2": nrm((DEPTH, DK), 0.1),
        "lambda_k2": nrm((DEPTH, DK), 0.1),
        "w_subln": 1.0 + nrm((DEPTH, DV), 0.02),
        "conv_w": nrm((DEPTH, CONV_W, CONV_DIM), CONV_W ** -0.5),
        "conv_b": nrm((DEPTH, CONV_DIM), 0.01),
        "dt_bias": dt_bias,
        "a_log": a_log,
        "d_skip": 1.0 + nrm((DEPTH, N_SSM_HEADS), 0.01),
        "w_ssm_norm": 1.0 + nrm((DEPTH, D_INNER), 0.02),
        "w_out": nrm((DEPTH, MIX_WIDTH, D_MODEL), MIX_WIDTH ** -0.5),
        "g_ffn": 1.0 + nrm((DEPTH, D_MODEL), 0.02),
        "w_up": nrm((DEPTH, D_MODEL, D_FF), D_MODEL ** -0.5),
        "w_down": nrm((DEPTH, D_FF, D_MODEL), 0.5 * D_FF ** -0.5),
        "w_ple": nrm((DEPTH, PLE_DIM, D_MODEL), PLE_DIM ** -0.5),
        "w_ple_norm": 1.0 + nrm((DEPTH, D_MODEL), 0.02),
        "w_ple_gate": nrm((DEPTH, D_MODEL, D_MODEL), D_MODEL ** -0.5),
        "g_final": 1.0 + nrm((D_MODEL,), 0.02),
    }


def reference(x_prompt, x_sample, p_prompt, p_sample, cache_k, cache_v, page_table, state_ssm, state_conv,
              g_mix, w_in, lambda_q1, lambda_k1, lambda_q2, lambda_k2, w_subln, conv_w, conv_b, dt_bias, a_log,
              d_skip, w_ssm_norm, w_out, g_ffn, w_up, w_down, w_ple, w_ple_norm, w_ple_gate, g_final):
    db = x_sample.shape[0]
    past = page_table.shape[1] * cache_k.shape[2]
    hp, hs = x_prompt, x_sample
    kp_l, vp_l, sp_l, cp_l, ks_l, vs_l, ss_l, cs_l = [], [], [], [], [], [], [], []
    for i in range(DEPTH):
        lam_init = 0.8 - 0.6 * math.exp(-0.3 * i)
        lam = (jnp.exp(jnp.sum(lambda_q1[i].astype(jnp.float32) * lambda_k1[i].astype(jnp.float32)))
               - jnp.exp(jnp.sum(lambda_q2[i].astype(jnp.float32) * lambda_k2[i].astype(jnp.float32)))
               + lam_init)

        u = _rmsnorm(hp, g_mix[i])
        q, k, v, z, xbc, dt = _split_proj(u, w_in[i])
        o_attn = _attn_out(_prompt_diff_attn(q, k, v, lam), w_subln[i], lam_init)
        xpad = jnp.pad(xbc, ((0, 0), (CONV_W - 1, 0), (0, 0)))
        xs, bm, cm, dtv, a = _ssm_inputs(_causal_conv(xpad, conv_w[i], conv_b[i]), dt, dt_bias[i], a_log[i])
        y, st = _ssd_chunked(xs, dtv, a, bm, cm)
        o_ssm = _ssm_out(y, xs, d_skip[i], z, w_ssm_norm[i])
        mix = jnp.concatenate([o_attn, o_ssm.astype(o_attn.dtype)], axis=-1)
        hp = _channel_and_ple(hp, mix, p_prompt[i], w_out[i], g_ffn[i], w_up[i], w_down[i],
                              w_ple[i], w_ple_norm[i], w_ple_gate[i])
        kp_l.append(k.reshape(*k.shape[:-2], 2 * DK))
        vp_l.append(v)
        sp_l.append(st)
        cp_l.append(xpad[:, -(CONV_W - 1):])

        u = _rmsnorm(hs, g_mix[i])
        q, k, v, z, xbc, dt = _split_proj(u, w_in[i])
        k_past = cache_k[i][page_table].reshape(db, past, N_ATTN_HEADS, 2, DK)
        v_past = cache_v[i][page_table].reshape(db, past, N_ATTN_HEADS, DV)
        o_attn = _attn_out(_sample_diff_attn(q, k, v, k_past, v_past, lam), w_subln[i], lam_init)
        xcat = jnp.concatenate([state_conv[i].astype(xbc.dtype), xbc], axis=1)
        xs, bm, cm, dtv, a = _ssm_inputs(_causal_conv(xcat, conv_w[i], conv_b[i]), dt, dt_bias[i], a_log[i])
        y, st = _ssm_recurrent(xs, dtv, a, bm, cm, state_ssm[i])
        o_ssm = _ssm_out(y, xs, d_skip[i], z, w_ssm_norm[i])
        mix = jnp.concatenate([o_attn, o_ssm.astype(o_attn.dtype)], axis=-1)
        hs = _channel_and_ple(hs, mix, p_sample[i], w_out[i], g_ffn[i], w_up[i], w_down[i],
                              w_ple[i], w_ple_norm[i], w_ple_gate[i])
        ks_l.append(k.reshape(*k.shape[:-2], 2 * DK))
        vs_l.append(v)
        ss_l.append(st)
        cs_l.append(xcat[:, -(CONV_W - 1):])

    y_prompt = _rmsnorm(hp, g_final)
    y_sample = _rmsnorm(hs, g_final)
    k_prompt = jnp.stack(kp_l)
    v_prompt = jnp.stack(vp_l)
    ssm_prompt = jnp.stack(sp_l)
    conv_prompt = jnp.stack(cp_l)
    k_sample = jnp.stack(ks_l)
    v_sample = jnp.stack(vs_l)
    ssm_sample = jnp.stack(ss_l)
    conv_sample = jnp.stack(cs_l)
    return (y_prompt, y_sample, k_prompt, v_prompt, ssm_prompt, conv_prompt, k_sample, v_sample, ssm_sample, conv_sample)
```

```python
import functools
import math

import jax
import jax.numpy as jnp
from jax import lax
from jax.experimental import pallas as pl
from jax.experimental.pallas import tpu as pltpu

F32 = jnp.float32
BF16 = jnp.bfloat16
NORM_EPS = 1e-6
SUBLN_EPS = 1e-5
HIGHEST = lax.Precision.HIGHEST

LANES = 128
SUBLANES = 8
VMEM_LIMIT_BYTES = 56 * 1024 * 1024


def _params(*semantics):
    return pltpu.CompilerParams(dimension_semantics=semantics, vmem_limit_bytes=VMEM_LIMIT_BYTES)


def _rms(x, eps):
    return x * lax.rsqrt(jnp.mean(x * x, axis=-1, keepdims=True) + eps)


def _silu(x):
    return x * jax.nn.sigmoid(x)


def _softplus(x):
    return jnp.maximum(x, 0.0) + jnp.log1p(jnp.exp(-jnp.abs(x)))


def _row_tile(rows, target):
    t = min(rows, target)
    assert rows % t == 0, (rows, t)
    return t


def _inproj_body(x_ref, g_ref, w_ref, wdt_ref, q_ref, k32_ref, v32_ref, kbf_ref, vbf_ref, z_ref,
                 xbc_ref, dt_ref, u_ref, *, bounds, q_scale):
    j = pl.program_id(1)

    @pl.when(j == 0)
    def _():
        u = _rms(x_ref[...], NORM_EPS) * g_ref[...]
        u_ref[...] = u.astype(BF16)
        dt_ref[...] = jnp.dot(u_ref[...], wdt_ref[...], preferred_element_type=F32)

    res = jnp.dot(u_ref[...], w_ref[...], preferred_element_type=F32)
    b_q, b_k, b_v, b_z = bounds

    @pl.when(j < b_q)
    def _():
        q_ref[...] = (res * q_scale).astype(BF16)

    @pl.when((j >= b_q) & (j < b_k))
    def _():
        k32_ref[...] = res
        kbf_ref[...] = res.astype(BF16)

    @pl.when((j >= b_k) & (j < b_v))
    def _():
        v32_ref[...] = res
        vbf_ref[...] = res.astype(BF16)

    @pl.when((j >= b_v) & (j < b_z))
    def _():
        z_ref[...] = res

    @pl.when(j >= b_z)
    def _():
        xbc_ref[...] = res


def _inproj(x, g, w_main, w_dt, *, widths, q_scale, tm_target=512, tn=512):
    rows, d = x.shape
    tm = _row_tile(rows, tm_target)
    aw, d_inner, conv_dim = widths
    seg = [aw, aw, aw, d_inner, conv_dim]
    assert all(s % tn == 0 for s in seg)
    tiles = [s // tn for s in seg]
    starts = [sum(tiles[:i]) for i in range(len(tiles))]
    n_tiles = sum(tiles)
    bounds = tuple(starts[i] + tiles[i] for i in range(4))

    def seg_map(s):
        lo, n = starts[s], tiles[s]
        return lambda i, j: (i, jnp.clip(j - lo, 0, n - 1))

    def out(s, dtype):
        return jax.ShapeDtypeStruct((rows, seg[s]), dtype), pl.BlockSpec((tm, tn), seg_map(s))

    outs = [out(0, BF16), out(1, F32), out(2, F32), out(1, BF16), out(2, BF16), out(3, F32), out(4, F32)]
    out_shape = [o[0] for o in outs] + [jax.ShapeDtypeStruct((rows, LANES), F32)]
    out_specs = [o[1] for o in outs] + [pl.BlockSpec((tm, LANES), lambda i, j: (i, 0))]
    return pl.pallas_call(
        functools.partial(_inproj_body, bounds=bounds, q_scale=q_scale),
        grid=(rows // tm, n_tiles),
        in_specs=[
            pl.BlockSpec((tm, d), lambda i, j: (i, 0)),
            pl.BlockSpec((1, d), lambda i, j: (0, 0)),
            pl.BlockSpec((d, tn), lambda i, j: (0, j)),
            pl.BlockSpec((d, LANES), lambda i, j: (0, 0)),
        ],
        out_specs=out_specs,
        out_shape=out_shape,
        scratch_shapes=[pltpu.VMEM((tm, d), BF16)],
        compiler_params=_params("arbitrary", "arbitrary"),
        name="norm_inproj",
    )(x, g, w_main, w_dt)


def _lambda_value(lq1_ref, lk1_ref, lq2_ref, lk2_ref, lam_init):
    s1 = jnp.sum(lq1_ref[...] * lk1_ref[...], axis=-1, keepdims=True)
    s2 = jnp.sum(lq2_ref[...] * lk2_ref[...], axis=-1, keepdims=True)
    return jnp.exp(s1) - jnp.exp(s2) + lam_init


def _attn_body(slopes_ref, lq1_ref, lk1_ref, lq2_ref, lk2_ref, wsub_ref, q_ref, k_ref, v_ref, o_ref,
               q2_ref, m_ref, l_ref, acc_ref, *, tq, dk, lam_init):
    h = pl.program_id(1)
    qi = pl.program_id(2)
    slope = slopes_ref[h]
    rows2 = 2 * tq
    row = lax.broadcasted_iota(jnp.int32, (rows2, tq), 0)
    col = lax.broadcasted_iota(jnp.int32, (rows2, tq), 1)
    rel = (jnp.where(row >= tq, row - tq, row) - col).astype(F32)
    tbias = -slope * rel

    q = q_ref[...]
    lane = lax.broadcasted_iota(jnp.int32, q.shape, 1)
    zero = jnp.zeros_like(q)
    q2_ref[0:tq, :] = jnp.where(lane < dk, q, zero)
    q2_ref[tq:rows2, :] = jnp.where(lane >= dk, q, zero)
    m_ref[...] = jnp.full(m_ref.shape, -jnp.inf, F32)
    l_ref[...] = jnp.zeros(l_ref.shape, F32)
    acc_ref[...] = jnp.zeros(acc_ref.shape, F32)

    def block(kb, masked):
        start = pl.multiple_of(kb * tq, tq)
        kblk = k_ref[pl.ds(start, tq), :]
        vblk = v_ref[pl.ds(start, tq), :]
        off = -slope * ((qi - kb) * tq).astype(F32)
        s = lax.dot_general(q2_ref[...], kblk, (((1,), (1,)), ((), ())), preferred_element_type=F32)
        s = s + tbias
        if masked:
            s = jnp.where(rel >= 0.0, s, -jnp.inf)
        m_old = m_ref[...]
        m_new = jnp.maximum(m_old, jnp.max(s, axis=-1, keepdims=True) + off)
        p = jnp.exp(s - (m_new - off))
        alpha = jnp.exp(m_old - m_new)
        l_ref[...] = alpha * l_ref[...] + jnp.sum(p, axis=-1, keepdims=True)
        acc_ref[...] = alpha * acc_ref[...] + jnp.dot(p.astype(BF16), vblk, preferred_element_type=F32)
        m_ref[...] = m_new

    def loop_body(kb, carry):
        block(kb, False)
        return carry

    lax.fori_loop(0, qi, loop_body, 0)
    block(qi, True)

    lam = _lambda_value(lq1_ref, lk1_ref, lq2_ref, lk2_ref, lam_init)
    o = acc_ref[...] / l_ref[...]
    o = o[0:tq] - lam * o[tq:rows2]
    o = _rms(o, SUBLN_EPS) * wsub_ref[...] * (1.0 - lam_init)
    o_ref[...] = o.astype(o_ref.dtype)


def _alibi_slopes(n_heads):
    return 2.0 ** (-8.0 * jnp.arange(1, n_heads + 1, dtype=F32) / n_heads)


def _prompt_attention(q_bf, k_bf, v_bf, lams, w_subln, *, batch, seq, n_heads, dk, dv, lam_init, tq=256):
    tq = _row_tile(seq, tq)
    nq = seq // tq
    lam_spec = pl.BlockSpec((1, dk), lambda b, h, i: (0, 0))
    return pl.pallas_call(
        functools.partial(_attn_body, tq=tq, dk=dk, lam_init=lam_init),
        grid=(batch, n_heads, nq),
        in_specs=[
            pl.BlockSpec(memory_space=pltpu.SMEM),
            lam_spec, lam_spec, lam_spec, lam_spec,
            pl.BlockSpec((1, dv), lambda b, h, i: (0, 0)),
            pl.BlockSpec((tq, 2 * dk), lambda b, h, i: (b * nq + i, h)),
            pl.BlockSpec((seq, 2 * dk), lambda b, h, i: (b, h)),
            pl.BlockSpec((seq, dv), lambda b, h, i: (b, h)),
        ],
        out_specs=pl.BlockSpec((tq, dv), lambda b, h, i: (b * nq + i, h)),
        out_shape=jax.ShapeDtypeStruct((batch * seq, n_heads * dv), BF16),
        scratch_shapes=[
            pltpu.VMEM((2 * tq, 2 * dk), BF16),
            pltpu.VMEM((2 * tq, 1), F32),
            pltpu.VMEM((2 * tq, 1), F32),
            pltpu.VMEM((2 * tq, dv), F32),
        ],
        compiler_params=_params("arbitrary", "arbitrary", "arbitrary"),
        name="prompt_diff_attention",
    )(_alibi_slopes(n_heads), *lams, w_subln, q_bf, k_bf, v_bf)


def _ssd_body(xbc_ref, z_ref, dt_ref, cw_ref, cb_ref, dtb_ref, alog_ref, dskip_ref, wn_ref, e_ref,
              o_ref, st_out_ref, ext_ref, st_ref, y_ref, *, chunk, conv_w, d_inner, d_state, n_groups,
              heads_per_group, headdim):
    c = pl.program_id(1)
    n_chunks = pl.num_programs(1)
    pad = SUBLANES
    gw = heads_per_group * headdim

    @pl.when(c == 0)
    def _():
        ext_ref[0:pad, :] = jnp.zeros((pad, ext_ref.shape[1]), F32)
        st_ref[...] = jnp.zeros(st_ref.shape, F32)

    ext_ref[pad:pad + chunk, :] = xbc_ref[...]
    conv = cb_ref[...]
    for j in range(conv_w):
        conv = conv + ext_ref[pl.ds(pad - (conv_w - 1) + j, chunk), :] * cw_ref[j:j + 1, :]
    ext_ref[0:pad, :] = ext_ref[chunk:chunk + pad, :]
    act = _silu(conv)

    xs = act[:, :d_inner]
    gn = n_groups * d_state
    bm = act[:, d_inner:d_inner + gn]
    cm = act[:, d_inner + gn:d_inner + 2 * gn]

    dtv = _softplus(dt_ref[...] + dtb_ref[...])
    da = dtv * (-jnp.exp(alog_ref[...]))
    r = lax.broadcasted_iota(jnp.int32, (chunk, chunk), 0)
    s = lax.broadcasted_iota(jnp.int32, (chunk, chunk), 1)
    causal = r >= s
    a_cs = jnp.dot(causal.astype(F32), da, precision=HIGHEST, preferred_element_type=F32)
    a_cs_t = a_cs.T

    expand = e_ref[...]
    dt_x = jnp.dot(dtv, expand, precision=HIGHEST, preferred_element_type=F32)
    acs_x = jnp.dot(a_cs, expand, precision=HIGHEST, preferred_element_type=F32)
    alast_x = acs_x[chunk - 1:chunk, :]
    xd = xs * dt_x
    xd_b = xd.astype(BF16)
    xdd_b = (xd * jnp.exp(alast_x - acs_x)).astype(BF16)
    eacs_x = jnp.exp(acs_x)
    chunk_decay_x = jnp.exp(alast_x)

    lane = lax.broadcasted_iota(jnp.int32, (chunk, LANES), 1)
    heads_per_tile = LANES // headdim
    for g in range(n_groups):
        bm_g = bm[:, g * d_state:(g + 1) * d_state]
        cm_b = cm[:, g * d_state:(g + 1) * d_state].astype(BF16)
        cb = lax.dot_general(cm_b, bm_g.astype(BF16), (((1,), (1,)), ((), ())), preferred_element_type=F32)
        gs = slice(g * gw, (g + 1) * gw)
        st_g = st_ref[g]
        y_off = jnp.dot(cm_b, st_g.astype(BF16), preferred_element_type=F32) * eacs_x[:, gs]
        new = jnp.dot(bm_g.T.astype(BF16), xdd_b[:, gs], preferred_element_type=F32)
        st_ref[g] = st_g * chunk_decay_x[:, gs] + new
        for t in range(gw // LANES):
            lo = g * gw + t * LANES
            xd_t = xd_b[:, lo:lo + LANES]
            y_t = y_off[:, t * LANES:(t + 1) * LANES]
            for k in range(heads_per_tile):
                hh = g * heads_per_group + t * heads_per_tile + k
                seg = a_cs[:, hh:hh + 1] - a_cs_t[hh:hh + 1, :]
                lmat = jnp.exp(jnp.where(causal, seg, -jnp.inf))
                in_head = (lane >= k * headdim) & (lane < (k + 1) * headdim)
                xd_h = jnp.where(in_head, xd_t, jnp.zeros_like(xd_t))
                y_t = y_t + jnp.dot((cb * lmat).astype(BF16), xd_h, preferred_element_type=F32)
            y_ref[:, lo:lo + LANES] = y_t

    y = y_ref[...] + dskip_ref[...] * xs
    gated = y * _silu(z_ref[...])
    wn = wn_ref[...]
    for g in range(n_groups):
        gs = slice(g * gw, (g + 1) * gw)
        o_ref[:, gs] = (_rms(gated[:, gs], NORM_EPS) * wn[:, gs]).astype(o_ref.dtype)

    @pl.when(c == n_chunks - 1)
    def _():
        for g in range(n_groups):
            st_out_ref[0, g * gw:(g + 1) * gw, :] = st_ref[g].T


def _head_expand(n_heads, headdim):
    head_of_lane = jnp.arange(n_heads * headdim) // headdim
    return (jnp.arange(LANES)[:, None] == head_of_lane[None, :]).astype(F32)


def _prompt_ssd(xbc, z, dt, conv_w, conv_b, dt_bias_p, a_log_p, dskip_x, w_norm, expand, *, batch, seq,
                d_state, n_groups, n_heads, headdim, chunk=128):
    chunk = _row_tile(seq, chunk)
    nc = seq // chunk
    conv_dim = xbc.shape[1]
    d_inner = n_heads * headdim
    cw = conv_w.shape[0]
    hpg = n_heads // n_groups
    const = lambda shape: pl.BlockSpec(shape, lambda b, c: (0, 0))
    row_map = lambda b, c: (b * nc + c, 0)
    return pl.pallas_call(
        functools.partial(_ssd_body, chunk=chunk, conv_w=cw, d_inner=d_inner, d_state=d_state,
                          n_groups=n_groups, heads_per_group=hpg, headdim=headdim),
        grid=(batch, nc),
        in_specs=[
            pl.BlockSpec((chunk, conv_dim), row_map),
            pl.BlockSpec((chunk, d_inner), row_map),
            pl.BlockSpec((chunk, LANES), row_map),
            const((cw, conv_dim)), const((1, conv_dim)), const((1, LANES)), const((1, LANES)),
            const((1, d_inner)), const((1, d_inner)), const((LANES, d_inner)),
        ],
        out_specs=[
            pl.BlockSpec((chunk, d_inner), row_map),
            pl.BlockSpec((1, d_inner, d_state), lambda b, c: (b, 0, 0)),
        ],
        out_shape=[
            jax.ShapeDtypeStruct((batch * seq, d_inner), BF16),
            jax.ShapeDtypeStruct((batch, d_inner, d_state), F32),
        ],
        scratch_shapes=[
            pltpu.VMEM((chunk + SUBLANES, conv_dim), F32),
            pltpu.VMEM((n_groups, d_state, hpg * headdim), F32),
            pltpu.VMEM((chunk, d_inner), F32),
        ],
        compiler_params=_params("arbitrary", "arbitrary"),
        name="prompt_ssd",
    )(xbc, z, dt, conv_w, conv_b, dt_bias_p, a_log_p, dskip_x, w_norm, expand)


def _outproj_body(oa_ref, os_ref, x_ref, wa_ref, ws_ref, g_ref, h_ref, u_ref):
    h = x_ref[...] + jnp.dot(oa_ref[...], wa_ref[...], preferred_element_type=F32)
    h = h + jnp.dot(os_ref[...], ws_ref[...], preferred_element_type=F32)
    h_ref[...] = h
    u_ref[...] = (_rms(h, NORM_EPS) * g_ref[...]).astype(u_ref.dtype)


def _outproj(o_attn, o_ssm, x, w_attn, w_ssm, g, *, tm_target=512):
    rows, d = x.shape
    tm = _row_tile(rows, tm_target)
    wa, ws = o_attn.shape[1], o_ssm.shape[1]
    row = lambda w: pl.BlockSpec((tm, w), lambda i: (i, 0))
    const = lambda shape: pl.BlockSpec(shape, lambda i: (0, 0))
    return pl.pallas_call(
        _outproj_body,
        grid=(rows // tm,),
        in_specs=[row(wa), row(ws), row(d), const((wa, d)), const((ws, d)), const((1, d))],
        out_specs=[row(d), row(d)],
        out_shape=[jax.ShapeDtypeStruct((rows, d), F32), jax.ShapeDtypeStruct((rows, d), BF16)],
        compiler_params=_params("arbitrary"),
        name="outproj_residual_norm",
    )(o_attn, o_ssm, x, w_attn, w_ssm, g)


def _mlp_body(u_ref, h_ref, wu_ref, wd_ref, o_ref):
    @pl.when(pl.program_id(1) == 0)
    def _():
        o_ref[...] = h_ref[...]

    a = jnp.dot(u_ref[...], wu_ref[...], preferred_element_type=F32)
    a = jnp.square(jnp.maximum(a, 0.0)).astype(BF16)
    o_ref[...] += jnp.dot(a, wd_ref[...], preferred_element_type=F32)


def _mlp(u, h, w_up, w_down, *, tm_target=512, tf=512):
    rows, d = h.shape
    d_ff = w_up.shape[1]
    tm = _row_tile(rows, tm_target)
    assert d_ff % tf == 0
    return pl.pallas_call(
        _mlp_body,
        grid=(rows // tm, d_ff // tf),
        in_specs=[
            pl.BlockSpec((tm, d), lambda i, f: (i, 0)),
            pl.BlockSpec((tm, d), lambda i, f: (i, 0)),
            pl.BlockSpec((d, tf), lambda i, f: (0, f)),
            pl.BlockSpec((tf, d), lambda i, f: (f, 0)),
        ],
        out_specs=pl.BlockSpec((tm, d), lambda i, f: (i, 0)),
        out_shape=jax.ShapeDtypeStruct((rows, d), F32),
        compiler_params=_params("arbitrary", "arbitrary"),
        name="relu2_mlp",
    )(u, h, w_up, w_down)


def _ple_body(h_ref, p_ref, wp_ref, wpn_ref, wg_ref, gf_ref, o_ref, *, final_norm):
    h = h_ref[...]
    e = jnp.dot(p_ref[...].astype(BF16), wp_ref[...], preferred_element_type=F32)
    e = _rms(e, NORM_EPS) * wpn_ref[...]
    gate = jax.nn.sigmoid(jnp.dot(h.astype(BF16), wg_ref[...], preferred_element_type=F32))
    h = h + e * gate
    if final_norm:
        h = _rms(h, NORM_EPS) * gf_ref[...]
    o_ref[...] = h


def _ple(h, p, w_ple, w_ple_norm, w_gate, g_final, *, final_norm, tm_target=512):
    rows, d = h.shape
    pd = p.shape[1]
    tm = _row_tile(rows, tm_target)
    row = lambda w: pl.BlockSpec((tm, w), lambda i: (i, 0))
    const = lambda shape: pl.BlockSpec(shape, lambda i: (0, 0))
    return pl.pallas_call(
        functools.partial(_ple_body, final_norm=final_norm),
        grid=(rows // tm,),
        in_specs=[row(d), row(pd), const((pd, d)), const((1, d)), const((d, d)), const((1, d))],
        out_specs=row(d),
        out_shape=jax.ShapeDtypeStruct((rows, d), F32),
        compiler_params=_params("arbitrary"),
        name="ple_gate",
    )(h, p, w_ple, w_ple_norm, w_gate, g_final)


def _paged_attn_body(pt_ref, lq1_ref, lk1_ref, lq2_ref, lk2_ref, wsub_ref, q_ref, kn_ref, vn_ref, *rest,
                     pages_per_step, n_heads, dk, dv, past_len, lam_init):
    del pt_ref
    k_refs = rest[:pages_per_step]
    v_refs = rest[pages_per_step:2 * pages_per_step]
    o_ref, wq_ref, m_ref, l_ref, acc_ref = rest[2 * pages_per_step:]
    g = pl.program_id(1)
    n_rows = 2 * n_heads
    width = q_ref.shape[-1]
    page = k_refs[0].shape[1]

    rowi = lax.broadcasted_iota(jnp.int32, (n_rows, 1), 0)
    head = jnp.where(rowi >= n_heads, rowi - n_heads, rowi)
    slope = jnp.exp2(-8.0 * (head + 1).astype(F32) / n_heads)

    @pl.when(g == 0)
    def _():
        r = lax.broadcasted_iota(jnp.int32, (n_rows, width), 0)
        c = lax.broadcasted_iota(jnp.int32, (n_rows, width), 1)
        r_head = jnp.where(r >= n_heads, r - n_heads, r)
        r_map = jnp.where(r >= n_heads, 1, 0)
        own = lax.shift_right_logical(c, int(math.log2(dk))) == (2 * r_head + r_map)
        q_rows = jnp.broadcast_to(q_ref[0].astype(F32), (n_rows, width))
        wq_ref[...] = jnp.where(own, q_rows, 0.0).astype(BF16)
        m_ref[...] = jnp.full(m_ref.shape, -jnp.inf, F32)
        l_ref[...] = jnp.zeros(l_ref.shape, F32)
        acc_ref[...] = jnp.zeros(acc_ref.shape, F32)

    tok = lax.broadcasted_iota(jnp.int32, (1, page), 1).astype(F32)
    wq = wq_ref[...]
    for r in range(pages_per_step):
        first = ((g * pages_per_step + r) * page).astype(F32)
        bias = -slope * ((past_len - first) - tok)
        kp = k_refs[r][0].astype(BF16)
        s = lax.dot_general(wq, kp, (((1,), (1,)), ((), ())), preferred_element_type=F32) + bias
        m_old = m_ref[...]
        m_new = jnp.maximum(m_old, jnp.max(s, axis=-1, keepdims=True))
        p = jnp.exp(s - m_new)
        alpha = jnp.exp(m_old - m_new)
        l_ref[...] = alpha * l_ref[...] + jnp.sum(p, axis=-1, keepdims=True)
        acc_ref[...] = alpha * acc_ref[...] + jnp.dot(p.astype(BF16), v_refs[r][0].astype(BF16),
                                                      preferred_element_type=F32)
        m_ref[...] = m_new

    @pl.when(g == pl.num_programs(1) - 1)
    def _():
        s_new = jnp.sum(wq_ref[...].astype(F32) * kn_ref[0], axis=-1, keepdims=True)
        m_old = m_ref[...]
        m_new = jnp.maximum(m_old, s_new)
        p_new = jnp.exp(s_new - m_new)
        alpha = jnp.exp(m_old - m_new)
        l_fin = alpha * l_ref[...] + p_new
        acc = alpha * acc_ref[...] + p_new * vn_ref[0]
        o_all = acc / l_fin
        hrow = lax.broadcasted_iota(jnp.int32, (n_heads, dv), 0)
        per_map = []
        for c in range(2):
            blk = jnp.zeros((n_heads, dv), F32)
            for hh in range(n_heads):
                blk = blk + jnp.where(hrow == hh, o_all[c * n_heads:(c + 1) * n_heads, hh * dv:(hh + 1) * dv], 0.0)
            per_map.append(blk)
        lam = _lambda_value(lq1_ref, lk1_ref, lq2_ref, lk2_ref, lam_init)
        o = per_map[0] - lam * per_map[1]
        o_ref[0] = (_rms(o, SUBLN_EPS) * wsub_ref[...] * (1.0 - lam_init)).astype(o_ref.dtype)


def _paged_attention(q_s, k_new, v_new, cache_k, cache_v, page_table, lams, w_subln, *, layer, n_heads, dk, dv,
                     lam_init, pages_per_step=4):
    db, n_pages = page_table.shape
    depth, n_pool, page, _, _ = cache_k.shape
    width = n_heads * dv
    assert n_pages % pages_per_step == 0
    ck = cache_k.reshape(depth * n_pool, page, width)
    cv = cache_v.reshape(depth * n_pool, page, width)
    base = layer * n_pool

    def page_spec(r):
        return pl.BlockSpec((1, page, width),
                            lambda b, g, pt: (base + pt[b * n_pages + g * pages_per_step + r], 0, 0))

    lam_spec = pl.BlockSpec((1, dk), lambda b, g, pt: (0, 0))
    tok_spec = pl.BlockSpec((1, 1, width), lambda b, g, pt: (b, 0, 0))
    grid_spec = pltpu.PrefetchScalarGridSpec(
        num_scalar_prefetch=1,
        grid=(db, n_pages // pages_per_step),
        in_specs=[lam_spec, lam_spec, lam_spec, lam_spec,
                  pl.BlockSpec((1, dv), lambda b, g, pt: (0, 0)),
                  tok_spec, tok_spec, tok_spec]
                 + [page_spec(r) for r in range(pages_per_step)]
                 + [page_spec(r) for r in range(pages_per_step)],
        out_specs=pl.BlockSpec((1, n_heads, dv), lambda b, g, pt: (b, 0, 0)),
        scratch_shapes=[
            pltpu.VMEM((2 * n_heads, width), BF16),
            pltpu.VMEM((2 * n_heads, 1), F32),
            pltpu.VMEM((2 * n_heads, 1), F32),
            pltpu.VMEM((2 * n_heads, width), F32),
        ],
    )
    out = pl.pallas_call(
        functools.partial(_paged_attn_body, pages_per_step=pages_per_step, n_heads=n_heads, dk=dk, dv=dv,
                          past_len=float(n_pages * page), lam_init=lam_init),
        grid_spec=grid_spec,
        out_shape=jax.ShapeDtypeStruct((db, n_heads, dv), BF16),
        compiler_params=_params("arbitrary", "arbitrary"),
        name="sample_paged_diff_attention",
    )(page_table.reshape(-1), *lams, w_subln, q_s.reshape(db, 1, width), k_new.reshape(db, 1, width),
      v_new.reshape(db, 1, width), *([ck] * pages_per_step), *([cv] * pages_per_step))
    return out.reshape(db, width)


def _to_column(row_vec, eye):
    n = eye.shape[0]
    return jnp.sum(jnp.where(eye, jnp.broadcast_to(row_vec, (n, n)), 0.0), axis=-1, keepdims=True)


def _to_row(col_vec, eye):
    n = eye.shape[0]
    return jnp.sum(jnp.where(eye, jnp.broadcast_to(col_vec, (n, n)), 0.0), axis=0, keepdims=True)


def _ssm_step_body(xbc_ref, z_ref, dt_ref, sc_ref, st_ref, cw_ref, cb_ref, dtb_ref, alog_ref, dskip_ref, wn_ref,
                   e_ref, o_ref, st_out_ref, *, conv_w, d_inner, d_state, n_groups, heads_per_group, headdim):
    gw = heads_per_group * headdim
    conv = cb_ref[...] + xbc_ref[0] * cw_ref[conv_w - 1:conv_w, :]
    sc = sc_ref[0]
    for j in range(conv_w - 1):
        conv = conv + sc[j:j + 1, :] * cw_ref[j:j + 1, :]
    act = _silu(conv)
    xs = act[:, :d_inner]
    gn = n_groups * d_state

    dtv = _softplus(dt_ref[0] + dtb_ref[...])
    da = dtv * (-jnp.exp(alog_ref[...]))
    expand = e_ref[...]
    dt_x = jnp.dot(dtv, expand, precision=HIGHEST, preferred_element_type=F32)
    decay_x = jnp.exp(jnp.dot(da, expand, precision=HIGHEST, preferred_element_type=F32))
    xd = xs * dt_x

    eye = (lax.broadcasted_iota(jnp.int32, (LANES, LANES), 0)
           == lax.broadcasted_iota(jnp.int32, (LANES, LANES), 1))
    y_tiles = []
    for t in range(d_inner // LANES):
        g = (t * LANES) // gw
        rows = slice(t * LANES, (t + 1) * LANES)
        b_row = act[:, d_inner + g * d_state:d_inner + (g + 1) * d_state]
        c_row = act[:, d_inner + gn + g * d_state:d_inner + gn + (g + 1) * d_state]
        st = st_ref[0, rows, :] * _to_column(decay_x[:, rows], eye) + _to_column(xd[:, rows], eye) * b_row
        st_out_ref[0, rows, :] = st
        y_tiles.append(_to_row(jnp.sum(st * c_row, axis=-1, keepdims=True), eye))
    y = jnp.concatenate(y_tiles, axis=-1) + dskip_ref[...] * xs
    gated = y * _silu(z_ref[0])
    wn = wn_ref[...]
    outs = []
    for g in range(n_groups):
        gs = slice(g * gw, (g + 1) * gw)
        outs.append(_rms(gated[:, gs], NORM_EPS) * wn[:, gs])
    o_ref[0] = jnp.concatenate(outs, axis=-1).astype(o_ref.dtype)


def _sample_ssm(xbc, z, dt, state_conv, state_ssm, conv_w, conv_b, dt_bias_p, a_log_p, dskip_x, w_norm, expand, *,
                d_state, n_groups, n_heads, headdim):
    db, conv_dim = xbc.shape
    d_inner = n_heads * headdim
    cw = conv_w.shape[0]
    per_b = lambda *tail: pl.BlockSpec((1,) + tail, lambda b: (b,) + (0,) * len(tail))
    const = lambda shape: pl.BlockSpec(shape, lambda b: (0, 0))
    o, st = pl.pallas_call(
        functools.partial(_ssm_step_body, conv_w=cw, d_inner=d_inner, d_state=d_state, n_groups=n_groups,
                          heads_per_group=n_heads // n_groups, headdim=headdim),
        grid=(db,),
        in_specs=[
            per_b(1, conv_dim), per_b(1, d_inner), per_b(1, LANES), per_b(cw - 1, conv_dim),
            per_b(d_inner, d_state),
            const((cw, conv_dim)), const((1, conv_dim)), const((1, LANES)), const((1, LANES)),
            const((1, d_inner)), const((1, d_inner)), const((LANES, d_inner)),
        ],
        out_specs=[per_b(1, d_inner), per_b(d_inner, d_state)],
        out_shape=[jax.ShapeDtypeStruct((db, 1, d_inner), BF16),
                   jax.ShapeDtypeStruct((db, d_inner, d_state), F32)],
        compiler_params=_params("arbitrary"),
        name="sample_ssm_step",
    )(xbc.reshape(db, 1, conv_dim), z.reshape(db, 1, d_inner), dt.reshape(db, 1, LANES), state_conv,
      state_ssm.reshape(db, d_inner, d_state), conv_w, conv_b, dt_bias_p, a_log_p, dskip_x, w_norm, expand)
    return o.reshape(db, d_inner), st


def _pad_lanes(v):
    return jnp.pad(v.astype(F32), (0, LANES - v.shape[0])).reshape(1, LANES)


def kernel(x_prompt, x_sample, p_prompt, p_sample, cache_k, cache_v, page_table, state_ssm, state_conv, g_mix, w_in, lambda_q1, lambda_k1, lambda_q2, lambda_k2, w_subln, conv_w, conv_b, dt_bias, a_log, d_skip, w_ssm_norm, w_out, g_ffn, w_up, w_down, w_ple, w_ple_norm, w_ple_gate, g_final):
    batch, seq, d_model = x_prompt.shape
    db, dec_seq, _ = x_sample.shape
    assert dec_seq == 1, "the sample kernels handle one new token per sequence"
    depth = w_in.shape[0]
    n_heads_a = cache_k.shape[3]
    dv = cache_v.shape[4]
    dk = lambda_q1.shape[1]
    assert cache_k.shape[4] == 2 * dk == dv == LANES
    aw = n_heads_a * dv
    n_heads_s, headdim, d_state = state_ssm.shape[2:]
    d_inner = n_heads_s * headdim
    conv_dim = state_conv.shape[3]
    n_groups = (conv_dim - d_inner) // (2 * d_state)
    assert n_heads_s <= LANES and LANES % headdim == 0 and d_state == LANES
    n_main = 3 * aw + d_inner + conv_dim
    q_scale = float(dk) ** -0.5
    expand = _head_expand(n_heads_s, headdim)

    hp = x_prompt.reshape(batch * seq, d_model)
    hs = x_sample.reshape(db, d_model)
    outs = {name: [] for name in ("kp", "vp", "sp", "cp", "ks", "vs", "ss", "cs")}
    for i in range(depth):
        lam_init = 0.8 - 0.6 * math.exp(-0.3 * i)
        last = i == depth - 1
        w_in_i = w_in[i].astype(BF16)
        w_main = w_in_i[:, :n_main]
        w_dt = jnp.pad(w_in_i[:, n_main:], ((0, 0), (0, LANES - n_heads_s)))
        w_out_i = w_out[i].astype(BF16)
        w_up_i, w_down_i = w_up[i].astype(BF16), w_down[i].astype(BF16)
        w_ple_i, w_gate_i = w_ple[i].astype(BF16), w_ple_gate[i].astype(BF16)
        g_mix_i, g_ffn_i = g_mix[i].reshape(1, -1), g_ffn[i].reshape(1, -1)
        lams = tuple(v[i].reshape(1, dk) for v in (lambda_q1, lambda_k1, lambda_q2, lambda_k2))
        w_subln_i = w_subln[i].reshape(1, dv)
        conv_b_i = conv_b[i].reshape(1, conv_dim)
        dt_bias_p, a_log_p = _pad_lanes(dt_bias[i]), _pad_lanes(a_log[i])
        dskip_x = jnp.repeat(d_skip[i].astype(F32), headdim).reshape(1, d_inner)
        w_norm_i = w_ssm_norm[i].reshape(1, d_inner)
        w_ple_norm_i = w_ple_norm[i].reshape(1, d_model)
        g_final_r = g_final.reshape(1, d_model)
        proj = functools.partial(_inproj, g=g_mix_i, w_main=w_main, w_dt=w_dt, widths=(aw, d_inner, conv_dim),
                                 q_scale=q_scale)
        ssm_kw = dict(d_state=d_state, n_groups=n_groups, n_heads=n_heads_s, headdim=headdim)

        def channel(h, o_attn, o_ssm, p):
            h1, u = _outproj(o_attn, o_ssm, h, w_out_i[:aw], w_out_i[aw:], g_ffn_i)
            h2 = _mlp(u, h1, w_up_i, w_down_i)
            return _ple(h2, p, w_ple_i, w_ple_norm_i, w_gate_i, g_final_r, final_norm=last)

        q_bf, k32, v32, k_bf, v_bf, z, xbc, dt = proj(hp)
        o_attn = _prompt_attention(q_bf, k_bf, v_bf, lams, w_subln_i, batch=batch, seq=seq, n_heads=n_heads_a,
                                   dk=dk, dv=dv, lam_init=lam_init)
        o_ssm, st = _prompt_ssd(xbc, z, dt, conv_w[i], conv_b_i, dt_bias_p, a_log_p, dskip_x, w_norm_i, expand,
                                batch=batch, seq=seq, **ssm_kw)
        hp = channel(hp, o_attn, o_ssm, p_prompt[i].reshape(batch * seq, -1))
        outs["kp"].append(k32.reshape(batch, seq, n_heads_a, 2 * dk))
        outs["vp"].append(v32.reshape(batch, seq, n_heads_a, dv))
        outs["sp"].append(st.reshape(batch, n_heads_s, headdim, d_state))
        cw = conv_w.shape[1]
        outs["cp"].append(xbc.reshape(batch, seq, conv_dim)[:, seq - (cw - 1):])

        q_bf, k32, v32, _, _, z, xbc, dt = proj(hs)
        o_attn = _paged_attention(q_bf, k32, v32, cache_k, cache_v, page_table, lams, w_subln_i, layer=i,
                                  n_heads=n_heads_a, dk=dk, dv=dv, lam_init=lam_init)
        o_ssm, st = _sample_ssm(xbc, z, dt, state_conv[i], state_ssm[i], conv_w[i], conv_b_i, dt_bias_p, a_log_p,
                                dskip_x, w_norm_i, expand, **ssm_kw)
        hs = channel(hs, o_attn, o_ssm, p_sample[i].reshape(db, -1))
        outs["ks"].append(k32.reshape(db, 1, n_heads_a, 2 * dk))
        outs["vs"].append(v32.reshape(db, 1, n_heads_a, dv))
        outs["ss"].append(st.reshape(db, n_heads_s, headdim, d_state))
        outs["cs"].append(jnp.concatenate([state_conv[i][:, 1:], xbc[:, None, :]], axis=1))

    stack = lambda name: jnp.stack(outs[name])
    return (hp.reshape(batch, seq, d_model), hs.reshape(db, 1, d_model), stack("kp"), stack("vp"), stack("sp"),
            stack("cp"), stack("ks"), stack("vs"), stack("ss"), stack("cs"))
```

```python
import functools
import math

import jax
import jax.numpy as jnp
import numpy as np
from jax import lax
from jax.experimental import pallas as pl
from jax.experimental.pallas import tpu as pltpu

F32 = jnp.float32
BF16 = jnp.bfloat16
NORM_EPS = 1e-6
SUBLN_EPS = 1e-5
HIGHEST = lax.Precision.HIGHEST
LOG2E = math.log2(math.e)

LANES = 128
SUBLANES = 8
VMEM_LIMIT_BYTES = 56 * 1024 * 1024


def _params(*semantics):
    return pltpu.CompilerParams(dimension_semantics=semantics, vmem_limit_bytes=VMEM_LIMIT_BYTES)


def _rms(x, eps):
    return x * lax.rsqrt(jnp.mean(x * x, axis=-1, keepdims=True) + eps)


def _silu(x):
    return x * jax.nn.sigmoid(x)


def _softplus(x):
    return jnp.maximum(x, 0.0) + jnp.log1p(jnp.exp(-jnp.abs(x)))


def _row_tile(rows, target):
    t = min(rows, target)
    assert rows % t == 0, (rows, t)
    return t


def _inproj_body(x_ref, g_ref, w_ref, wdt_ref, q_ref, k32_ref, v32_ref, kbf_ref, vbf_ref, z_ref,
                 xbc_ref, dt_ref, u_ref, *, bounds, q_scale):
    j = pl.program_id(1)

    @pl.when(j == 0)
    def _():
        u = _rms(x_ref[...], NORM_EPS) * g_ref[...]
        u_ref[...] = u.astype(BF16)
        dt_ref[...] = jnp.dot(u_ref[...], wdt_ref[...], preferred_element_type=F32)

    res = jnp.dot(u_ref[...], w_ref[...], preferred_element_type=F32)
    b_q, b_k, b_v, b_z = bounds

    @pl.when(j < b_q)
    def _():
        q_ref[...] = (res * q_scale).astype(BF16)

    @pl.when((j >= b_q) & (j < b_k))
    def _():
        k32_ref[...] = res
        kbf_ref[...] = res.astype(BF16)

    @pl.when((j >= b_k) & (j < b_v))
    def _():
        v32_ref[...] = res
        vbf_ref[...] = res.astype(BF16)

    @pl.when((j >= b_v) & (j < b_z))
    def _():
        z_ref[...] = res

    @pl.when(j >= b_z)
    def _():
        xbc_ref[...] = res


def _inproj(x, g, w_main, w_dt, *, widths, q_scale, tm_target=1024, tn=512):
    rows, d = x.shape
    tm = _row_tile(rows, tm_target)
    aw, d_inner, conv_dim = widths
    seg = [aw, aw, aw, d_inner, conv_dim]
    assert all(s % tn == 0 for s in seg)
    tiles = [s // tn for s in seg]
    starts = [sum(tiles[:i]) for i in range(len(tiles))]
    n_tiles = sum(tiles)
    bounds = tuple(starts[i] + tiles[i] for i in range(4))

    def seg_map(s):
        lo, n = starts[s], tiles[s]
        return lambda i, j: (i, jnp.clip(j - lo, 0, n - 1))

    def out(s, dtype):
        return jax.ShapeDtypeStruct((rows, seg[s]), dtype), pl.BlockSpec((tm, tn), seg_map(s))

    outs = [out(0, BF16), out(1, F32), out(2, F32), out(1, BF16), out(2, BF16), out(3, F32), out(4, F32)]
    out_shape = [o[0] for o in outs] + [jax.ShapeDtypeStruct((rows, LANES), F32)]
    out_specs = [o[1] for o in outs] + [pl.BlockSpec((tm, LANES), lambda i, j: (i, 0))]
    return pl.pallas_call(
        functools.partial(_inproj_body, bounds=bounds, q_scale=q_scale),
        grid=(rows // tm, n_tiles),
        in_specs=[
            pl.BlockSpec((tm, d), lambda i, j: (i, 0)),
            pl.BlockSpec((1, d), lambda i, j: (0, 0)),
            pl.BlockSpec((d, tn), lambda i, j: (0, j)),
            pl.BlockSpec((d, LANES), lambda i, j: (0, 0)),
        ],
        out_specs=out_specs,
        out_shape=out_shape,
        scratch_shapes=[pltpu.VMEM((tm, d), BF16)],
        compiler_params=_params("arbitrary", "arbitrary"),
        name="norm_inproj",
    )(x, g, w_main, w_dt)


def _lambda_value(lq1_ref, lk1_ref, lq2_ref, lk2_ref, lam_init):
    s1 = jnp.sum(lq1_ref[...] * lk1_ref[...], axis=-1, keepdims=True)
    s2 = jnp.sum(lq2_ref[...] * lk2_ref[...], axis=-1, keepdims=True)
    return jnp.exp(s1) - jnp.exp(s2) + lam_init


def _attn_body(coef_ref, lq1_ref, lk1_ref, lq2_ref, lk2_ref, wsub_ref, qf_ref, kf_ref, q_ref, k_ref, v_ref, o_ref,
               vt_ref, q2_ref, m_ref, acc_ref, *, tq, dk, dv, heads, lam_init):
    hg = pl.program_id(1)
    qi = pl.program_id(2)
    cols2 = 2 * tq
    n_kb, tk = vt_ref.shape[1], vt_ref.shape[3]
    assert tk == tq
    dvx = vt_ref.shape[2]

    @pl.when(qi == 0)
    def _():
        for hb in range(heads):
            for kb in range(n_kb):
                for part in range(tk // LANES):
                    rows = slice(kb * tk + part * LANES, kb * tk + (part + 1) * LANES)
                    vt = v_ref[rows, hb * dv:(hb + 1) * dv].astype(F32).T
                    vt_ref[hb, kb, 0:dv, part * LANES:(part + 1) * LANES] = vt.astype(BF16)
                vt_ref[hb, kb, dv:dvx, :] = jnp.ones((dvx - dv, tk), BF16)
            q2_ref[hb, 0:2 * dk, :] = jnp.zeros((2 * dk, cols2), BF16)
            q2_ref[hb, 2 * dk:, 0:tq] = qf_ref[hb]
            q2_ref[hb, 2 * dk:, tq:cols2] = qf_ref[hb]

    for hb in range(heads):
        qt = q_ref[:, hb * 2 * dk:(hb + 1) * 2 * dk].astype(F32).T.astype(BF16)
        q2_ref[hb, 0:dk, 0:tq] = qt[0:dk]
        q2_ref[hb, dk:2 * dk, tq:cols2] = qt[dk:2 * dk]
    m_ref[...] = jnp.full(m_ref.shape, -jnp.inf, F32)
    acc_ref[...] = jnp.zeros(acc_ref.shape, F32)

    def block(hb, kb, masked):
        start = pl.multiple_of(kb * tk, tk)
        kx = jnp.concatenate([k_ref[pl.ds(start, tk), hb * 2 * dk:(hb + 1) * 2 * dk], kf_ref[hb]], axis=-1)
        off = -coef_ref[hg * heads + hb] * ((qi - kb) * tq).astype(F32)
        s = jnp.dot(kx, q2_ref[hb], preferred_element_type=F32)
        if masked:
            key = lax.broadcasted_iota(jnp.int32, (tk, cols2), 0)
            col = lax.broadcasted_iota(jnp.int32, (tk, cols2), 1)
            s = jnp.where(jnp.where(col >= tq, col - tq, col) >= key, s, -jnp.inf)
        m_old = m_ref[hb]
        m_new = jnp.maximum(m_old, jnp.max(s, axis=0, keepdims=True) + off)
        p = jnp.exp2(s - (m_new - off)).astype(BF16)
        alpha = jnp.exp2(m_old - m_new)
        acc_ref[hb] = alpha * acc_ref[hb] + jnp.dot(vt_ref[hb, kb], p, preferred_element_type=F32)
        m_ref[hb] = m_new

    def loop_body(kb, carry):
        for hb in range(heads):
            block(hb, kb, False)
        return carry

    lax.fori_loop(0, qi, loop_body, 0)
    for hb in range(heads):
        block(hb, qi, True)

    lam = _lambda_value(lq1_ref, lk1_ref, lq2_ref, lk2_ref, lam_init)
    for hb in range(heads):
        acc = acc_ref[hb]
        o = acc[0:dv] / acc[dv:dv + 1]
        o = o[:, 0:tq] - lam * o[:, tq:cols2]
        o = o * lax.rsqrt(jnp.mean(o * o, axis=0, keepdims=True) + SUBLN_EPS)
        o = o * wsub_ref[...] * (1.0 - lam_init)
        o_ref[:, hb * dv:(hb + 1) * dv] = o.T.astype(o_ref.dtype)


def _bf16_split3(v):
    hi = v.astype(BF16)
    r1 = v - hi.astype(np.float32)
    mid = r1.astype(BF16)
    lo = (r1 - mid.astype(np.float32)).astype(BF16)
    return [hi, mid, lo]


def _alibi_lanes(n_heads, t):
    slopes = np.float32(2.0) ** (np.float32(-8.0) * np.arange(1, n_heads + 1, dtype=np.float32) / np.float32(n_heads))
    coef = (slopes * np.float32(LOG2E)).astype(np.float32)
    cpos = coef[:, None] * np.arange(t, dtype=np.float32)[None, :]
    ones = [np.ones((n_heads, t), BF16)] * 3
    pad = [np.zeros((n_heads, t), BF16)] * (LANES - 6)
    q_lanes = np.stack(_bf16_split3(-cpos) + ones + pad, axis=1)
    k_lanes = np.stack(ones + _bf16_split3(cpos) + pad, axis=-1)
    return jnp.asarray(coef), jnp.asarray(q_lanes), jnp.asarray(k_lanes)


def _prompt_attention(q_bf, k_bf, v_bf, lams, w_subln, *, batch, seq, n_heads, dk, dv, lam_init, tq=256, heads=4):
    tq = _row_tile(seq, tq)
    nq = seq // tq
    assert n_heads % heads == 0
    coef, q_lanes, k_lanes = _alibi_lanes(n_heads, tq)
    ones_rows = 2 * SUBLANES
    lam_spec = pl.BlockSpec((1, dk), lambda b, h, i: (0, 0))
    qlanes_spec = pl.BlockSpec((heads, LANES, tq), lambda b, h, i: (h, 0, 0))
    klanes_spec = pl.BlockSpec((heads, tq, LANES), lambda b, h, i: (h, 0, 0))
    return pl.pallas_call(
        functools.partial(_attn_body, tq=tq, dk=dk, dv=dv, heads=heads, lam_init=lam_init),
        grid=(batch, n_heads // heads, nq),
        in_specs=[
            pl.BlockSpec(memory_space=pltpu.SMEM),
            lam_spec, lam_spec, lam_spec, lam_spec,
            pl.BlockSpec((dv, 1), lambda b, h, i: (0, 0)),
            qlanes_spec, klanes_spec,
            pl.BlockSpec((tq, heads * 2 * dk), lambda b, h, i: (b * nq + i, h)),
            pl.BlockSpec((seq, heads * 2 * dk), lambda b, h, i: (b, h)),
            pl.BlockSpec((seq, heads * dv), lambda b, h, i: (b, h)),
        ],
        out_specs=pl.BlockSpec((tq, heads * dv), lambda b, h, i: (b * nq + i, h)),
        out_shape=jax.ShapeDtypeStruct((batch * seq, n_heads * dv), BF16),
        scratch_shapes=[
            pltpu.VMEM((heads, nq, dv + ones_rows, tq), BF16),
            pltpu.VMEM((heads, 2 * dk + LANES, 2 * tq), BF16),
            pltpu.VMEM((heads, 1, 2 * tq), F32),
            pltpu.VMEM((heads, dv + ones_rows, 2 * tq), F32),
        ],
        compiler_params=_params("arbitrary", "arbitrary", "arbitrary"),
        name="prompt_diff_attention",
    )(coef, *lams, w_subln.reshape(dv, 1), q_lanes, k_lanes, q_bf, k_bf, v_bf)


def _ssd_body(xbc_ref, z_ref, dt_ref, cw_ref, cb_ref, dtb_ref, alog_ref, dskip_ref, wn_ref, e_ref,
              o_ref, st_out_ref, ext_ref, st_ref, y_ref, *, chunk, conv_w, d_inner, d_state, n_groups,
              heads_per_group, headdim):
    c = pl.program_id(1)
    n_chunks = pl.num_programs(1)
    pad = SUBLANES
    gw = heads_per_group * headdim

    @pl.when(c == 0)
    def _():
        ext_ref[0:pad, :] = jnp.zeros((pad, ext_ref.shape[1]), F32)
        st_ref[...] = jnp.zeros(st_ref.shape, F32)

    ext_ref[pad:pad + chunk, :] = xbc_ref[...]
    conv = cb_ref[...]
    for j in range(conv_w):
        conv = conv + ext_ref[pl.ds(pad - (conv_w - 1) + j, chunk), :] * cw_ref[j:j + 1, :]
    ext_ref[0:pad, :] = ext_ref[chunk:chunk + pad, :]
    act = _silu(conv)

    xs = act[:, :d_inner]
    gn = n_groups * d_state
    bm = act[:, d_inner:d_inner + gn]
    cm = act[:, d_inner + gn:d_inner + 2 * gn]

    dtv = _softplus(dt_ref[...] + dtb_ref[...])
    da = dtv * (-jnp.exp(alog_ref[...]))
    r = lax.broadcasted_iota(jnp.int32, (chunk, chunk), 0)
    s = lax.broadcasted_iota(jnp.int32, (chunk, chunk), 1)
    causal = r >= s
    a_cs = jnp.dot(causal.astype(F32), da, precision=HIGHEST, preferred_element_type=F32)
    a_cs_t = a_cs.T

    expand = e_ref[...]
    dt_x = jnp.dot(dtv, expand, precision=HIGHEST, preferred_element_type=F32)
    acs_x = jnp.dot(a_cs, expand, precision=HIGHEST, preferred_element_type=F32)
    alast_x = acs_x[chunk - 1:chunk, :]
    xd = xs * dt_x
    xd_b = xd.astype(BF16)
    xdd_b = (xd * jnp.exp(alast_x - acs_x)).astype(BF16)
    eacs_x = jnp.exp(acs_x)
    chunk_decay_x = jnp.exp(alast_x)

    lane = lax.broadcasted_iota(jnp.int32, (chunk, LANES), 1)
    heads_per_tile = LANES // headdim
    for g in range(n_groups):
        bm_g = bm[:, g * d_state:(g + 1) * d_state]
        cm_b = cm[:, g * d_state:(g + 1) * d_state].astype(BF16)
        cb = lax.dot_general(cm_b, bm_g.astype(BF16), (((1,), (1,)), ((), ())), preferred_element_type=F32)
        gs = slice(g * gw, (g + 1) * gw)
        st_g = st_ref[g]
        y_off = jnp.dot(cm_b, st_g.astype(BF16), preferred_element_type=F32) * eacs_x[:, gs]
        new = jnp.dot(bm_g.T.astype(BF16), xdd_b[:, gs], preferred_element_type=F32)
        st_ref[g] = st_g * chunk_decay_x[:, gs] + new
        for t in range(gw // LANES):
            lo = g * gw + t * LANES
            xd_t = xd_b[:, lo:lo + LANES]
            y_t = y_off[:, t * LANES:(t + 1) * LANES]
            for k in range(heads_per_tile):
                hh = g * heads_per_group + t * heads_per_tile + k
                seg = a_cs[:, hh:hh + 1] - a_cs_t[hh:hh + 1, :]
                lmat = jnp.exp(jnp.where(causal, seg, -jnp.inf))
                in_head = (lane >= k * headdim) & (lane < (k + 1) * headdim)
                xd_h = jnp.where(in_head, xd_t, jnp.zeros_like(xd_t))
                y_t = y_t + jnp.dot((cb * lmat).astype(BF16), xd_h, preferred_element_type=F32)
            y_ref[:, lo:lo + LANES] = y_t

    y = y_ref[...] + dskip_ref[...] * xs
    gated = y * _silu(z_ref[...])
    wn = wn_ref[...]
    for g in range(n_groups):
        gs = slice(g * gw, (g + 1) * gw)
        o_ref[:, gs] = (_rms(gated[:, gs], NORM_EPS) * wn[:, gs]).astype(o_ref.dtype)

    @pl.when(c == n_chunks - 1)
    def _():
        for g in range(n_groups):
            st_out_ref[0, g * gw:(g + 1) * gw, :] = st_ref[g].T


def _head_expand(n_heads, headdim):
    head_of_lane = jnp.arange(n_heads * headdim) // headdim
    return (jnp.arange(LANES)[:, None] == head_of_lane[None, :]).astype(F32)


def _prompt_ssd(xbc, z, dt, conv_w, conv_b, dt_bias_p, a_log_p, dskip_x, w_norm, expand, *, batch, seq,
                d_state, n_groups, n_heads, headdim, chunk=128):
    chunk = _row_tile(seq, chunk)
    nc = seq // chunk
    conv_dim = xbc.shape[1]
    d_inner = n_heads * headdim
    cw = conv_w.shape[0]
    hpg = n_heads // n_groups
    const = lambda shape: pl.BlockSpec(shape, lambda b, c: (0, 0))
    row_map = lambda b, c: (b * nc + c, 0)
    return pl.pallas_call(
        functools.partial(_ssd_body, chunk=chunk, conv_w=cw, d_inner=d_inner, d_state=d_state,
                          n_groups=n_groups, heads_per_group=hpg, headdim=headdim),
        grid=(batch, nc),
        in_specs=[
            pl.BlockSpec((chunk, conv_dim), row_map),
            pl.BlockSpec((chunk, d_inner), row_map),
            pl.BlockSpec((chunk, LANES), row_map),
            const((cw, conv_dim)), const((1, conv_dim)), const((1, LANES)), const((1, LANES)),
            const((1, d_inner)), const((1, d_inner)), const((LANES, d_inner)),
        ],
        out_specs=[
            pl.BlockSpec((chunk, d_inner), row_map),
            pl.BlockSpec((1, d_inner, d_state), lambda b, c: (b, 0, 0)),
        ],
        out_shape=[
            jax.ShapeDtypeStruct((batch * seq, d_inner), BF16),
            jax.ShapeDtypeStruct((batch, d_inner, d_state), F32),
        ],
        scratch_shapes=[
            pltpu.VMEM((chunk + SUBLANES, conv_dim), F32),
            pltpu.VMEM((n_groups, d_state, hpg * headdim), F32),
            pltpu.VMEM((chunk, d_inner), F32),
        ],
        compiler_params=_params("arbitrary", "arbitrary"),
        name="prompt_ssd",
    )(xbc, z, dt, conv_w, conv_b, dt_bias_p, a_log_p, dskip_x, w_norm, expand)


def _outproj_body(oa_ref, os_ref, x_ref, wa_ref, ws_ref, g_ref, h_ref, u_ref):
    h = x_ref[...] + jnp.dot(oa_ref[...], wa_ref[...], preferred_element_type=F32)
    h = h + jnp.dot(os_ref[...], ws_ref[...], preferred_element_type=F32)
    h_ref[...] = h
    u_ref[...] = (_rms(h, NORM_EPS) * g_ref[...]).astype(u_ref.dtype)


def _outproj(o_attn, o_ssm, x, w_out, g, *, tm_target=512):
    rows, d = x.shape
    tm = _row_tile(rows, tm_target)
    wa, ws = o_attn.shape[1], o_ssm.shape[1]
    assert wa == ws and w_out.shape[0] == wa + ws
    row = lambda w: pl.BlockSpec((tm, w), lambda i: (i, 0))
    return pl.pallas_call(
        _outproj_body,
        grid=(rows // tm,),
        in_specs=[row(wa), row(ws), row(d),
                  pl.BlockSpec((wa, d), lambda i: (0, 0)), pl.BlockSpec((ws, d), lambda i: (1, 0)),
                  pl.BlockSpec((1, d), lambda i: (0, 0))],
        out_specs=[row(d), row(d)],
        out_shape=[jax.ShapeDtypeStruct((rows, d), F32), jax.ShapeDtypeStruct((rows, d), BF16)],
        compiler_params=_params("arbitrary"),
        name="outproj_residual_norm",
    )(o_attn, o_ssm, x, w_out, w_out, g)


def _mlp_body(u_ref, h_ref, wu_ref, wd_ref, o_ref):
    @pl.when(pl.program_id(1) == 0)
    def _():
        o_ref[...] = h_ref[...]

    a = jnp.dot(u_ref[...], wu_ref[...], preferred_element_type=F32)
    a = jnp.square(jnp.maximum(a, 0.0)).astype(BF16)
    o_ref[...] += jnp.dot(a, wd_ref[...], preferred_element_type=F32)


def _mlp(u, h, w_up, w_down, *, tm_target=1024, tf=512):
    rows, d = h.shape
    d_ff = w_up.shape[1]
    tm = _row_tile(rows, tm_target)
    assert d_ff % tf == 0
    return pl.pallas_call(
        _mlp_body,
        grid=(rows // tm, d_ff // tf),
        in_specs=[
            pl.BlockSpec((tm, d), lambda i, f: (i, 0)),
            pl.BlockSpec((tm, d), lambda i, f: (i, 0)),
            pl.BlockSpec((d, tf), lambda i, f: (0, f)),
            pl.BlockSpec((tf, d), lambda i, f: (f, 0)),
        ],
        out_specs=pl.BlockSpec((tm, d), lambda i, f: (i, 0)),
        out_shape=jax.ShapeDtypeStruct((rows, d), F32),
        compiler_params=_params("arbitrary", "arbitrary"),
        name="relu2_mlp",
    )(u, h, w_up, w_down)


def _ple_body(h_ref, p_ref, wp_ref, wpn_ref, wg_ref, gf_ref, o_ref, *, final_norm):
    h = h_ref[...]
    e = jnp.dot(p_ref[...].astype(BF16), wp_ref[...], preferred_element_type=F32)
    e = _rms(e, NORM_EPS) * wpn_ref[...]
    gate = jax.nn.sigmoid(jnp.dot(h.astype(BF16), wg_ref[...], preferred_element_type=F32))
    h = h + e * gate
    if final_norm:
        h = _rms(h, NORM_EPS) * gf_ref[...]
    o_ref[...] = h


def _ple(h, p, w_ple, w_ple_norm, w_gate, g_final, *, final_norm, tm_target=512):
    rows, d = h.shape
    pd = p.shape[1]
    tm = _row_tile(rows, tm_target)
    row = lambda w: pl.BlockSpec((tm, w), lambda i: (i, 0))
    const = lambda shape: pl.BlockSpec(shape, lambda i: (0, 0))
    return pl.pallas_call(
        functools.partial(_ple_body, final_norm=final_norm),
        grid=(rows // tm,),
        in_specs=[row(d), row(pd), const((pd, d)), const((1, d)), const((d, d)), const((1, d))],
        out_specs=row(d),
        out_shape=jax.ShapeDtypeStruct((rows, d), F32),
        compiler_params=_params("arbitrary"),
        name="ple_gate",
    )(h, p, w_ple, w_ple_norm, w_gate, g_final)


def _paged_attn_body(pt_ref, lq1_ref, lk1_ref, lq2_ref, lk2_ref, wsub_ref, q_ref, kn_ref, vn_ref, *rest,
                     pages_per_step, n_heads, dk, page, lam_init):
    del pt_ref
    k_refs = rest[:pages_per_step]
    v_refs = rest[pages_per_step:2 * pages_per_step]
    o_ref, qm_ref, base_ref, m_ref, l_ref, acc_ref = rest[2 * pages_per_step:]
    g = pl.program_id(1)
    n_rows = 2 * n_heads
    n_cols = page * n_heads
    head_bits = int(math.log2(n_heads))
    past_len = pl.num_programs(1) * pages_per_step * page

    rowi = lax.broadcasted_iota(jnp.int32, (n_rows, 1), 0)
    head = jnp.where(rowi >= n_heads, rowi - n_heads, rowi)
    coef = jnp.exp2(-8.0 * (head + 1).astype(F32) / n_heads) * LOG2E

    @pl.when(g == 0)
    def _():
        q = q_ref[0].astype(F32)
        lane = lax.broadcasted_iota(jnp.int32, q.shape, 1)
        qm = jnp.concatenate([jnp.where(lane < dk, q, 0.0), jnp.where(lane >= dk, q, 0.0)], axis=0)
        qm_ref[...] = qm.astype(BF16)
        col = lax.broadcasted_iota(jnp.int32, (n_rows, n_cols), 1)
        own = (col & (n_heads - 1)) == head
        tok = lax.shift_right_logical(col, head_bits)
        dist = (past_len - tok).astype(F32)
        base_ref[...] = jnp.where(own, -coef * dist, -jnp.inf)
        m_ref[...] = jnp.full(m_ref.shape, -jnp.inf, F32)
        l_ref[...] = jnp.zeros(l_ref.shape, F32)
        acc_ref[...] = jnp.zeros(acc_ref.shape, F32)

    qm = qm_ref[...]
    scores, offs, top = [], [], None
    for r in range(pages_per_step):
        off = coef * ((g * pages_per_step + r) * page).astype(F32)
        kp = k_refs[r][...].astype(BF16)
        s = lax.dot_general(qm, kp, (((1,), (1,)), ((), ())), preferred_element_type=F32) + base_ref[...]
        cand = jnp.max(s, axis=-1, keepdims=True) + off
        top = cand if top is None else jnp.maximum(top, cand)
        scores.append(s)
        offs.append(off)
    m_old = m_ref[...]
    m_new = jnp.maximum(m_old, top)
    alpha = jnp.exp2(m_old - m_new)
    l_new = alpha * l_ref[...]
    acc_new = alpha * acc_ref[...]
    for r in range(pages_per_step):
        p = jnp.exp2(scores[r] - (m_new - offs[r]))
        l_new = l_new + jnp.sum(p, axis=-1, keepdims=True)
        acc_new = acc_new + jnp.dot(p.astype(BF16), v_refs[r][...].astype(BF16), preferred_element_type=F32)
    l_ref[...] = l_new
    acc_ref[...] = acc_new
    m_ref[...] = m_new

    @pl.when(g == pl.num_programs(1) - 1)
    def _():
        kn = jnp.concatenate([kn_ref[0], kn_ref[0]], axis=0)
        vn = jnp.concatenate([vn_ref[0], vn_ref[0]], axis=0)
        s_new = jnp.sum(qm_ref[...].astype(F32) * kn, axis=-1, keepdims=True)
        m_old = m_ref[...]
        m_new = jnp.maximum(m_old, s_new)
        p_new = jnp.exp2(s_new - m_new)
        alpha = jnp.exp2(m_old - m_new)
        o_all = (alpha * acc_ref[...] + p_new * vn) / (alpha * l_ref[...] + p_new)
        lam = _lambda_value(lq1_ref, lk1_ref, lq2_ref, lk2_ref, lam_init)
        o = o_all[0:n_heads] - lam * o_all[n_heads:n_rows]
        o_ref[0] = (_rms(o, SUBLN_EPS) * wsub_ref[...] * (1.0 - lam_init)).astype(o_ref.dtype)


def _paged_attention(q_s, k_new, v_new, cache_k, cache_v, page_table, lams, w_subln, *, layer, n_heads, dk, dv,
                     lam_init, pages_per_step=8):
    db, n_pages = page_table.shape
    depth, n_pool, page, _, _ = cache_k.shape
    assert n_pages % pages_per_step == 0 and n_heads & (n_heads - 1) == 0
    rows = page * n_heads
    ck = cache_k.reshape(depth * n_pool * rows, 2 * dk)
    cv = cache_v.reshape(depth * n_pool * rows, dv)
    base = layer * n_pool

    def page_spec(r, width):
        return pl.BlockSpec((rows, width), lambda b, g, pt: (base + pt[b * n_pages + g * pages_per_step + r], 0))

    lam_spec = pl.BlockSpec((1, dk), lambda b, g, pt: (0, 0))
    tok_spec = pl.BlockSpec((1, n_heads, dv), lambda b, g, pt: (b, 0, 0))
    n_rows = 2 * n_heads
    grid_spec = pltpu.PrefetchScalarGridSpec(
        num_scalar_prefetch=1,
        grid=(db, n_pages // pages_per_step),
        in_specs=[lam_spec, lam_spec, lam_spec, lam_spec,
                  pl.BlockSpec((1, dv), lambda b, g, pt: (0, 0)),
                  tok_spec, tok_spec, tok_spec]
                 + [page_spec(r, 2 * dk) for r in range(pages_per_step)]
                 + [page_spec(r, dv) for r in range(pages_per_step)],
        out_specs=tok_spec,
        scratch_shapes=[
            pltpu.VMEM((n_rows, 2 * dk), BF16),
            pltpu.VMEM((n_rows, rows), F32),
            pltpu.VMEM((n_rows, 1), F32),
            pltpu.VMEM((n_rows, 1), F32),
            pltpu.VMEM((n_rows, dv), F32),
        ],
    )
    heads = lambda a: a.reshape(db, n_heads, dv)
    out = pl.pallas_call(
        functools.partial(_paged_attn_body, pages_per_step=pages_per_step, n_heads=n_heads, dk=dk, page=page,
                          lam_init=lam_init),
        grid_spec=grid_spec,
        out_shape=jax.ShapeDtypeStruct((db, n_heads, dv), BF16),
        compiler_params=_params("arbitrary", "arbitrary"),
        name="sample_paged_diff_attention",
    )(page_table.reshape(-1), *lams, w_subln, heads(q_s), heads(k_new), heads(v_new),
      *([ck] * pages_per_step), *([cv] * pages_per_step))
    return out.reshape(db, n_heads * dv)


def _to_column(row_vec, eye):
    n = eye.shape[0]
    return jnp.sum(jnp.where(eye, jnp.broadcast_to(row_vec, (n, n)), 0.0), axis=-1, keepdims=True)


def _to_row(col_vec, eye):
    n = eye.shape[0]
    return jnp.sum(jnp.where(eye, jnp.broadcast_to(col_vec, (n, n)), 0.0), axis=0, keepdims=True)


def _ssm_step_body(xbc_ref, z_ref, dt_ref, sc_ref, st_ref, cw_ref, cb_ref, dtb_ref, alog_ref, dskip_ref, wn_ref,
                   e_ref, o_ref, st_out_ref, *, conv_w, d_inner, d_state, n_groups, heads_per_group, headdim):
    gw = heads_per_group * headdim
    conv = cb_ref[...] + xbc_ref[0] * cw_ref[conv_w - 1:conv_w, :]
    sc = sc_ref[0]
    for j in range(conv_w - 1):
        conv = conv + sc[j:j + 1, :] * cw_ref[j:j + 1, :]
    act = _silu(conv)
    xs = act[:, :d_inner]
    gn = n_groups * d_state

    dtv = _softplus(dt_ref[0] + dtb_ref[...])
    da = dtv * (-jnp.exp(alog_ref[...]))
    expand = e_ref[...]
    dt_x = jnp.dot(dtv, expand, precision=HIGHEST, preferred_element_type=F32)
    decay_x = jnp.exp(jnp.dot(da, expand, precision=HIGHEST, preferred_element_type=F32))
    xd = xs * dt_x

    eye = (lax.broadcasted_iota(jnp.int32, (LANES, LANES), 0)
           == lax.broadcasted_iota(jnp.int32, (LANES, LANES), 1))
    y_tiles = []
    for t in range(d_inner // LANES):
        g = (t * LANES) // gw
        rows = slice(t * LANES, (t + 1) * LANES)
        b_row = act[:, d_inner + g * d_state:d_inner + (g + 1) * d_state]
        c_row = act[:, d_inner + gn + g * d_state:d_inner + gn + (g + 1) * d_state]
        st = st_ref[0, rows, :] * _to_column(decay_x[:, rows], eye) + _to_column(xd[:, rows], eye) * b_row
        st_out_ref[0, rows, :] = st
        y_tiles.append(_to_row(jnp.sum(st * c_row, axis=-1, keepdims=True), eye))
    y = jnp.concatenate(y_tiles, axis=-1) + dskip_ref[...] * xs
    gated = y * _silu(z_ref[0])
    wn = wn_ref[...]
    outs = []
    for g in range(n_groups):
        gs = slice(g * gw, (g + 1) * gw)
        outs.append(_rms(gated[:, gs], NORM_EPS) * wn[:, gs])
    o_ref[0] = jnp.concatenate(outs, axis=-1).astype(o_ref.dtype)


def _sample_ssm(xbc, z, dt, state_conv, state_ssm, conv_w, conv_b, dt_bias_p, a_log_p, dskip_x, w_norm, expand, *,
                d_state, n_groups, n_heads, headdim):
    db, conv_dim = xbc.shape
    d_inner = n_heads * headdim
    cw = conv_w.shape[0]
    per_b = lambda *tail: pl.BlockSpec((1,) + tail, lambda b: (b,) + (0,) * len(tail))
    const = lambda shape: pl.BlockSpec(shape, lambda b: (0, 0))
    o, st = pl.pallas_call(
        functools.partial(_ssm_step_body, conv_w=cw, d_inner=d_inner, d_state=d_state, n_groups=n_groups,
                          heads_per_group=n_heads // n_groups, headdim=headdim),
        grid=(db,),
        in_specs=[
            per_b(1, conv_dim), per_b(1, d_inner), per_b(1, LANES), per_b(cw - 1, conv_dim),
            per_b(d_inner, d_state),
            const((cw, conv_dim)), const((1, conv_dim)), const((1, LANES)), const((1, LANES)),
            const((1, d_inner)), const((1, d_inner)), const((LANES, d_inner)),
        ],
        out_specs=[per_b(1, d_inner), per_b(d_inner, d_state)],
        out_shape=[jax.ShapeDtypeStruct((db, 1, d_inner), BF16),
                   jax.ShapeDtypeStruct((db, d_inner, d_state), F32)],
        compiler_params=_params("arbitrary"),
        name="sample_ssm_step",
    )(xbc.reshape(db, 1, conv_dim), z.reshape(db, 1, d_inner), dt.reshape(db, 1, LANES), state_conv,
      state_ssm.reshape(db, d_inner, d_state), conv_w, conv_b, dt_bias_p, a_log_p, dskip_x, w_norm, expand)
    return o.reshape(db, d_inner), st


def _pad_lanes(v):
    return jnp.pad(v.astype(F32), (0, LANES - v.shape[0])).reshape(1, LANES)


def kernel(x_prompt, x_sample, p_prompt, p_sample, cache_k, cache_v, page_table, state_ssm, state_conv, g_mix, w_in, lambda_q1, lambda_k1, lambda_q2, lambda_k2, w_subln, conv_w, conv_b, dt_bias, a_log, d_skip, w_ssm_norm, w_out, g_ffn, w_up, w_down, w_ple, w_ple_norm, w_ple_gate, g_final):
    batch, seq, d_model = x_prompt.shape
    db, dec_seq, _ = x_sample.shape
    assert dec_seq == 1, "the sample kernels handle one new token per sequence"
    depth = w_in.shape[0]
    n_heads_a = cache_k.shape[3]
    dv = cache_v.shape[4]
    dk = lambda_q1.shape[1]
    assert cache_k.shape[4] == 2 * dk == dv == LANES
    aw = n_heads_a * dv
    n_heads_s, headdim, d_state = state_ssm.shape[2:]
    d_inner = n_heads_s * headdim
    conv_dim = state_conv.shape[3]
    n_groups = (conv_dim - d_inner) // (2 * d_state)
    assert n_heads_s <= LANES and LANES % headdim == 0 and d_state == LANES
    n_main = 3 * aw + d_inner + conv_dim
    q_scale = float(dk) ** -0.5 * LOG2E
    expand = _head_expand(n_heads_s, headdim)

    hp = x_prompt.reshape(batch * seq, d_model)
    hs = x_sample.reshape(db, d_model)
    outs = {name: [] for name in ("kp", "vp", "sp", "cp", "ks", "vs", "ss", "cs")}
    for i in range(depth):
        lam_init = 0.8 - 0.6 * math.exp(-0.3 * i)
        last = i == depth - 1
        w_main = w_in[i].astype(BF16)
        w_dt = jnp.pad(w_in[i][:, n_main:].astype(BF16), ((0, 0), (0, LANES - n_heads_s)))
        w_out_i = w_out[i].astype(BF16)
        w_up_i, w_down_i = w_up[i].astype(BF16), w_down[i].astype(BF16)
        w_ple_i, w_gate_i = w_ple[i].astype(BF16), w_ple_gate[i].astype(BF16)
        g_mix_i, g_ffn_i = g_mix[i].reshape(1, -1), g_ffn[i].reshape(1, -1)
        lams = tuple(v[i].reshape(1, dk) for v in (lambda_q1, lambda_k1, lambda_q2, lambda_k2))
        w_subln_i = w_subln[i].reshape(1, dv)
        conv_b_i = conv_b[i].reshape(1, conv_dim)
        dt_bias_p, a_log_p = _pad_lanes(dt_bias[i]), _pad_lanes(a_log[i])
        dskip_x = jnp.repeat(d_skip[i].astype(F32), headdim).reshape(1, d_inner)
        w_norm_i = w_ssm_norm[i].reshape(1, d_inner)
        w_ple_norm_i = w_ple_norm[i].reshape(1, d_model)
        g_final_r = g_final.reshape(1, d_model)
        proj = functools.partial(_inproj, g=g_mix_i, w_main=w_main, w_dt=w_dt, widths=(aw, d_inner, conv_dim),
                                 q_scale=q_scale)
        ssm_kw = dict(d_state=d_state, n_groups=n_groups, n_heads=n_heads_s, headdim=headdim)

        def channel(h, o_attn, o_ssm, p):
            h1, u = _outproj(o_attn, o_ssm, h, w_out_i, g_ffn_i)
            h2 = _mlp(u, h1, w_up_i, w_down_i)
            return _ple(h2, p, w_ple_i, w_ple_norm_i, w_gate_i, g_final_r, final_norm=last)

        q_bf, k32, v32, k_bf, v_bf, z, xbc, dt = proj(hp)
        o_attn = _prompt_attention(q_bf, k_bf, v_bf, lams, w_subln_i, batch=batch, seq=seq, n_heads=n_heads_a,
                                   dk=dk, dv=dv, lam_init=lam_init)
        o_ssm, st = _prompt_ssd(xbc, z, dt, conv_w[i], conv_b_i, dt_bias_p, a_log_p, dskip_x, w_norm_i, expand,
                                batch=batch, seq=seq, **ssm_kw)
        hp = channel(hp, o_attn, o_ssm, p_prompt[i].reshape(batch * seq, -1))
        outs["kp"].append(k32.reshape(batch, seq, n_heads_a, 2 * dk))
        outs["vp"].append(v32.reshape(batch, seq, n_heads_a, dv))
        outs["sp"].append(st.reshape(batch, n_heads_s, headdim, d_state))
        cw = conv_w.shape[1]
        outs["cp"].append(xbc.reshape(batch, seq, conv_dim)[:, seq - (cw - 1):])

        q_bf, k32, v32, _, _, z, xbc, dt = proj(hs)
        o_attn = _paged_attention(q_bf, k32, v32, cache_k, cache_v, page_table, lams, w_subln_i, layer=i,
                                  n_heads=n_heads_a, dk=dk, dv=dv, lam_init=lam_init)
        o_ssm, st = _sample_ssm(xbc, z, dt, state_conv[i], state_ssm[i], conv_w[i], conv_b_i, dt_bias_p, a_log_p,
                                dskip_x, w_norm_i, expand, **ssm_kw)
        hs = channel(hs, o_attn, o_ssm, p_sample[i].reshape(db, -1))
        outs["ks"].append(k32.reshape(db, 1, n_heads_a, 2 * dk))
        outs["vs"].append(v32.reshape(db, 1, n_heads_a, dv))
        outs["ss"].append(st.reshape(db, n_heads_s, headdim, d_state))
        outs["cs"].append(jnp.concatenate([state_conv[i][:, 1:], xbc[:, None, :]], axis=1))

    stack = lambda name: jnp.stack(outs[name])
    return (hp.reshape(batch, seq, d_model), hs.reshape(db, 1, d_model), stack("kp"), stack("vp"), stack("sp"),
            stack("cp"), stack("ks"), stack("vs"), stack("ss"), stack("cs"))
```

```python
import functools
import math

import jax
import jax.numpy as jnp
import numpy as np
from jax import lax
from jax.experimental import pallas as pl
from jax.experimental.pallas import tpu as pltpu

F32 = jnp.float32
BF16 = jnp.bfloat16
NORM_EPS = 1e-6
SUBLN_EPS = 1e-5
HIGHEST = lax.Precision.HIGHEST
LOG2E = math.log2(math.e)

LANES = 128
SUBLANES = 8
VMEM_LIMIT_BYTES = 56 * 1024 * 1024


def _params(*semantics):
    return pltpu.CompilerParams(dimension_semantics=semantics, vmem_limit_bytes=VMEM_LIMIT_BYTES)


def _rms(x, eps):
    return x * lax.rsqrt(jnp.mean(x * x, axis=-1, keepdims=True) + eps)


def _silu(x):
    h = 0.5 * x
    return h + h * jnp.tanh(h)


def _softplus(x):
    return jnp.maximum(x, 0.0) + jnp.log1p(jnp.exp(-jnp.abs(x)))


def _row_tile(rows, target):
    t = min(rows, target)
    assert rows % t == 0, (rows, t)
    return t


def _inproj_body(x_ref, g_ref, w_ref, wdt_ref, q_ref, k32_ref, v32_ref, kbf_ref, vbf_ref, z_ref,
                 xbc_ref, dt_ref, u_ref, *, bounds, q_scale):
    j = pl.program_id(1)

    @pl.when(j == 0)
    def _():
        u = _rms(x_ref[...], NORM_EPS) * g_ref[...]
        u_ref[...] = u.astype(BF16)
        dt_ref[...] = jnp.dot(u_ref[...], wdt_ref[...], preferred_element_type=F32)

    def tile():
        return jnp.dot(u_ref[...], w_ref[...], preferred_element_type=F32)

    b_q, b_k, b_v, b_z = bounds

    @pl.when(j < b_q)
    def _():
        q_ref[...] = (tile() * q_scale).astype(BF16)

    @pl.when((j >= b_q) & (j < b_k))
    def _():
        res = tile()
        k32_ref[...] = res
        kbf_ref[...] = res.astype(BF16)

    @pl.when((j >= b_k) & (j < b_v))
    def _():
        res = tile()
        v32_ref[...] = res
        vbf_ref[...] = res.astype(BF16)

    @pl.when((j >= b_v) & (j < b_z))
    def _():
        z_ref[...] = tile()

    @pl.when(j >= b_z)
    def _():
        xbc_ref[...] = tile()


def _inproj(x, g, w_main, w_dt, *, widths, q_scale, tm_target=1024, tn=512):
    rows, d = x.shape
    tm = _row_tile(rows, tm_target)
    aw, d_inner, conv_dim = widths
    seg = [aw, aw, aw, d_inner, conv_dim]
    assert all(s % tn == 0 for s in seg)
    tiles = [s // tn for s in seg]
    starts = [sum(tiles[:i]) for i in range(len(tiles))]
    n_tiles = sum(tiles)
    bounds = tuple(starts[i] + tiles[i] for i in range(4))

    def seg_map(s):
        lo, n = starts[s], tiles[s]
        return lambda i, j: (i, jnp.clip(j - lo, 0, n - 1))

    def out(s, dtype):
        return jax.ShapeDtypeStruct((rows, seg[s]), dtype), pl.BlockSpec((tm, tn), seg_map(s))

    outs = [out(0, BF16), out(1, F32), out(2, F32), out(1, BF16), out(2, BF16), out(3, F32), out(4, F32)]
    out_shape = [o[0] for o in outs] + [jax.ShapeDtypeStruct((rows, LANES), F32)]
    out_specs = [o[1] for o in outs] + [pl.BlockSpec((tm, LANES), lambda i, j: (i, 0))]
    return pl.pallas_call(
        functools.partial(_inproj_body, bounds=bounds, q_scale=q_scale),
        grid=(rows // tm, n_tiles),
        in_specs=[
            pl.BlockSpec((tm, d), lambda i, j: (i, 0)),
            pl.BlockSpec((1, d), lambda i, j: (0, 0)),
            pl.BlockSpec((d, tn), lambda i, j: (0, j)),
            pl.BlockSpec((d, LANES), lambda i, j: (0, 0)),
        ],
        out_specs=out_specs,
        out_shape=out_shape,
        scratch_shapes=[pltpu.VMEM((tm, d), BF16)],
        compiler_params=_params("arbitrary", "arbitrary"),
        name="norm_inproj",
    )(x, g, w_main, w_dt)


def _lambda_value(lq1_ref, lk1_ref, lq2_ref, lk2_ref, lam_init):
    s1 = jnp.sum(lq1_ref[...] * lk1_ref[...], axis=-1, keepdims=True)
    s2 = jnp.sum(lq2_ref[...] * lk2_ref[...], axis=-1, keepdims=True)
    return jnp.exp(s1) - jnp.exp(s2) + lam_init


def _attn_body(coef_ref, lq1_ref, lk1_ref, lq2_ref, lk2_ref, wsub_ref, qf_ref, kf_ref, q_ref, k_ref, v_ref, o_ref,
               vt_ref, q2_ref, s_ref, m_ref, acc_ref, *, tq, dk, dv, heads, lam_init):
    hg = pl.program_id(1)
    qi = pl.program_id(2)
    cols2 = 2 * tq
    n_kb, tk = vt_ref.shape[1], vt_ref.shape[3]
    assert tk == tq
    dvx = vt_ref.shape[2]

    @pl.when(qi == 0)
    def _():
        for hb in range(heads):
            for kb in range(n_kb):
                for part in range(tk // LANES):
                    rows = slice(kb * tk + part * LANES, kb * tk + (part + 1) * LANES)
                    vt = v_ref[rows, hb * dv:(hb + 1) * dv].astype(F32).T
                    vt_ref[hb, kb, 0:dv, part * LANES:(part + 1) * LANES] = vt.astype(BF16)
                vt_ref[hb, kb, dv:dvx, :] = jnp.ones((dvx - dv, tk), BF16)
            q2_ref[hb, 0:2 * dk, :] = jnp.zeros((2 * dk, cols2), BF16)
            q2_ref[hb, 2 * dk:, 0:tq] = qf_ref[hb]
            q2_ref[hb, 2 * dk:, tq:cols2] = qf_ref[hb]

    for hb in range(heads):
        qt = q_ref[:, hb * 2 * dk:(hb + 1) * 2 * dk].astype(F32).T.astype(BF16)
        q2_ref[hb, 0:dk, 0:tq] = qt[0:dk]
        q2_ref[hb, dk:2 * dk, tq:cols2] = qt[dk:2 * dk]
    m_ref[...] = jnp.full(m_ref.shape, -jnp.inf, F32)
    acc_ref[...] = jnp.zeros(acc_ref.shape, F32)

    def score(hb, kb, slot):
        start = pl.multiple_of(kb * tk, tk)
        kx = jnp.concatenate([k_ref[pl.ds(start, tk), hb * 2 * dk:(hb + 1) * 2 * dk], kf_ref[hb]], axis=-1)
        s_ref[hb, slot] = jnp.dot(kx, q2_ref[hb], preferred_element_type=F32)

    def accumulate(hb, kb, slot, masked):
        s = s_ref[hb, slot]
        off = -coef_ref[hg * heads + hb] * ((qi - kb) * tq).astype(F32)
        if masked:
            key = lax.broadcasted_iota(jnp.int32, (tk, cols2), 0)
            col = lax.broadcasted_iota(jnp.int32, (tk, cols2), 1)
            s = jnp.where(jnp.where(col >= tq, col - tq, col) >= key, s, -jnp.inf)
        m_old = m_ref[hb]
        m_new = jnp.maximum(m_old, jnp.max(s, axis=0, keepdims=True) + off)
        p = jnp.exp2(s - (m_new - off)).astype(BF16)
        alpha = jnp.exp2(m_old - m_new)
        acc_ref[hb] = alpha * acc_ref[hb] + jnp.dot(vt_ref[hb, kb], p, preferred_element_type=F32)
        m_ref[hb] = m_new

    def stage(kb_next, kb, slot, masked=False):
        for hb in range(heads):
            if kb_next is not None:
                score(hb, kb_next, 1 - slot)
            accumulate(hb, kb, slot, masked)

    def pair(j, carry):
        stage(2 * j + 1, 2 * j, 0)
        stage(2 * j + 2, 2 * j + 1, 1)
        return carry

    for hb in range(heads):
        score(hb, 0, 0)
    lax.fori_loop(0, qi // 2, pair, 0)

    @pl.when(qi % 2 == 0)
    def _():
        stage(None, qi, 0, masked=True)

    @pl.when(qi % 2 == 1)
    def _():
        stage(qi, qi - 1, 0)
        stage(None, qi, 1, masked=True)

    lam = _lambda_value(lq1_ref, lk1_ref, lq2_ref, lk2_ref, lam_init)
    for hb in range(heads):
        acc = acc_ref[hb]
        o = acc[0:dv] / acc[dv:dv + 1]
        o = o[:, 0:tq] - lam * o[:, tq:cols2]
        o = o * lax.rsqrt(jnp.mean(o * o, axis=0, keepdims=True) + SUBLN_EPS)
        o = o * wsub_ref[...] * (1.0 - lam_init)
        o_ref[:, hb * dv:(hb + 1) * dv] = o.T.astype(o_ref.dtype)


def _bf16_split3(v):
    hi = v.astype(BF16)
    r1 = v - hi.astype(np.float32)
    mid = r1.astype(BF16)
    lo = (r1 - mid.astype(np.float32)).astype(BF16)
    return [hi, mid, lo]


def _alibi_lanes(n_heads, t):
    slopes = np.float32(2.0) ** (np.float32(-8.0) * np.arange(1, n_heads + 1, dtype=np.float32) / np.float32(n_heads))
    coef = (slopes * np.float32(LOG2E)).astype(np.float32)
    cpos = coef[:, None] * np.arange(t, dtype=np.float32)[None, :]
    ones = [np.ones((n_heads, t), BF16)] * 3
    pad = [np.zeros((n_heads, t), BF16)] * (LANES - 6)
    q_lanes = np.stack(_bf16_split3(-cpos) + ones + pad, axis=1)
    k_lanes = np.stack(ones + _bf16_split3(cpos) + pad, axis=-1)
    return jnp.asarray(coef), jnp.asarray(q_lanes), jnp.asarray(k_lanes)


def _prompt_attention(q_bf, k_bf, v_bf, lams, w_subln, *, batch, seq, n_heads, dk, dv, lam_init, tq=256, heads=4):
    tq = _row_tile(seq, tq)
    nq = seq // tq
    assert n_heads % heads == 0
    coef, q_lanes, k_lanes = _alibi_lanes(n_heads, tq)
    ones_rows = 2 * SUBLANES
    lam_spec = pl.BlockSpec((1, dk), lambda b, h, i: (0, 0))
    qlanes_spec = pl.BlockSpec((heads, LANES, tq), lambda b, h, i: (h, 0, 0))
    klanes_spec = pl.BlockSpec((heads, tq, LANES), lambda b, h, i: (h, 0, 0))
    return pl.pallas_call(
        functools.partial(_attn_body, tq=tq, dk=dk, dv=dv, heads=heads, lam_init=lam_init),
        grid=(batch, n_heads // heads, nq),
        in_specs=[
            pl.BlockSpec(memory_space=pltpu.SMEM),
            lam_spec, lam_spec, lam_spec, lam_spec,
            pl.BlockSpec((dv, 1), lambda b, h, i: (0, 0)),
            qlanes_spec, klanes_spec,
            pl.BlockSpec((tq, heads * 2 * dk), lambda b, h, i: (b * nq + i, h)),
            pl.BlockSpec((seq, heads * 2 * dk), lambda b, h, i: (b, h)),
            pl.BlockSpec((seq, heads * dv), lambda b, h, i: (b, h)),
        ],
        out_specs=pl.BlockSpec((tq, heads * dv), lambda b, h, i: (b * nq + i, h)),
        out_shape=jax.ShapeDtypeStruct((batch * seq, n_heads * dv), BF16),
        scratch_shapes=[
            pltpu.VMEM((heads, nq, dv + ones_rows, tq), BF16),
            pltpu.VMEM((heads, 2 * dk + LANES, 2 * tq), BF16),
            pltpu.VMEM((heads, 2, tq, 2 * tq), F32),
            pltpu.VMEM((heads, 1, 2 * tq), F32),
            pltpu.VMEM((heads, dv + ones_rows, 2 * tq), F32),
        ],
        compiler_params=_params("arbitrary", "arbitrary", "arbitrary"),
        name="prompt_diff_attention",
    )(coef, *lams, w_subln.reshape(dv, 1), q_lanes, k_lanes, q_bf, k_bf, v_bf)


def _ssd_body(xbc_ref, z_ref, dt_ref, cw_ref, cb_ref, dtb_ref, alog_ref, dskip_ref, wn_ref, e_ref,
              o_ref, st_out_ref, ext_ref, st_ref, y_ref, *, chunk, conv_w, d_inner, d_state, n_groups,
              heads_per_group, headdim):
    c = pl.program_id(1)
    n_chunks = pl.num_programs(1)
    pad = SUBLANES
    gw = heads_per_group * headdim

    @pl.when(c == 0)
    def _():
        ext_ref[0:pad, :] = jnp.zeros((pad, ext_ref.shape[1]), F32)
        st_ref[...] = jnp.zeros(st_ref.shape, F32)

    ext_ref[pad:pad + chunk, :] = xbc_ref[...]
    ext = ext_ref[...]
    conv = cb_ref[...] + ext[pad:pad + chunk] * cw_ref[conv_w - 1:conv_w, :]
    for j in range(conv_w - 1):
        shifted = pltpu.roll(ext, conv_w - 1 - j, axis=0)
        conv = conv + shifted[pad:pad + chunk] * cw_ref[j:j + 1, :]
    ext_ref[0:pad, :] = ext[chunk:chunk + pad]
    act = _silu(conv)

    xs = act[:, :d_inner]
    gn = n_groups * d_state
    bm = act[:, d_inner:d_inner + gn]
    cm = act[:, d_inner + gn:d_inner + 2 * gn]

    dtv = _softplus(dt_ref[...] + dtb_ref[...])
    da = dtv * (-jnp.exp(alog_ref[...]))
    r = lax.broadcasted_iota(jnp.int32, (chunk, chunk), 0)
    s = lax.broadcasted_iota(jnp.int32, (chunk, chunk), 1)
    causal = r >= s
    tri = causal.astype(F32).astype(BF16)
    a_cs = jnp.dot(jnp.concatenate([tri, tri, tri], axis=1), jnp.concatenate(_split3(da), axis=0),
                   preferred_element_type=F32)
    a_cs_t = a_cs.T

    both = jnp.concatenate([dtv, a_cs], axis=0)
    both_x = jnp.dot(jnp.concatenate(_split3(both), axis=1), e_ref[...], preferred_element_type=F32)
    dt_x = both_x[0:chunk]
    acs_x = both_x[chunk:2 * chunk]
    alast_x = acs_x[chunk - 1:chunk, :]
    xd = xs * dt_x
    xd_b = xd.astype(BF16)
    xdd_b = (xd * jnp.exp(alast_x - acs_x)).astype(BF16)
    eacs_x = jnp.exp(acs_x)
    chunk_decay_x = jnp.exp(alast_x)

    lane = lax.broadcasted_iota(jnp.int32, (chunk, LANES), 1)
    heads_per_tile = LANES // headdim
    for g in range(n_groups):
        bm_g = bm[:, g * d_state:(g + 1) * d_state]
        cm_b = cm[:, g * d_state:(g + 1) * d_state].astype(BF16)
        cb = lax.dot_general(cm_b, bm_g.astype(BF16), (((1,), (1,)), ((), ())), preferred_element_type=F32)
        gs = slice(g * gw, (g + 1) * gw)
        st_g = st_ref[g]
        y_off = jnp.dot(cm_b, st_g.astype(BF16), preferred_element_type=F32) * eacs_x[:, gs]
        new = jnp.dot(bm_g.T.astype(BF16), xdd_b[:, gs], preferred_element_type=F32)
        st_ref[g] = st_g * chunk_decay_x[:, gs] + new
        for t in range(gw // LANES):
            lo = g * gw + t * LANES
            xd_t = xd_b[:, lo:lo + LANES]
            y_t = y_off[:, t * LANES:(t + 1) * LANES]
            for k in range(heads_per_tile):
                hh = g * heads_per_group + t * heads_per_tile + k
                seg = a_cs[:, hh:hh + 1] - a_cs_t[hh:hh + 1, :]
                lmat = jnp.exp(jnp.where(causal, seg, -jnp.inf))
                in_head = (lane >= k * headdim) & (lane < (k + 1) * headdim)
                xd_h = jnp.where(in_head, xd_t, jnp.zeros_like(xd_t))
                y_t = y_t + jnp.dot((cb * lmat).astype(BF16), xd_h, preferred_element_type=F32)
            y_ref[:, lo:lo + LANES] = y_t

    y = y_ref[...] + dskip_ref[...] * xs
    gated = y * _silu(z_ref[...])
    wn = wn_ref[...]
    for g in range(n_groups):
        gs = slice(g * gw, (g + 1) * gw)
        o_ref[:, gs] = (_rms(gated[:, gs], NORM_EPS) * wn[:, gs]).astype(o_ref.dtype)

    @pl.when(c == n_chunks - 1)
    def _():
        for g in range(n_groups):
            st_out_ref[0, g * gw:(g + 1) * gw, :] = st_ref[g].T


def _split3(x):
    hi = x.astype(BF16)
    r1 = x - hi.astype(F32)
    mid = r1.astype(BF16)
    lo = (r1 - mid.astype(F32)).astype(BF16)
    return [hi, mid, lo]


def _head_expand(n_heads, headdim):
    head_of_lane = jnp.arange(n_heads * headdim) // headdim
    return (jnp.arange(LANES)[:, None] == head_of_lane[None, :]).astype(F32)


def _prompt_ssd(xbc, z, dt, conv_w, conv_b, dt_bias_p, a_log_p, dskip_x, w_norm, expand, *, batch, seq,
                d_state, n_groups, n_heads, headdim, chunk=128):
    chunk = _row_tile(seq, chunk)
    nc = seq // chunk
    conv_dim = xbc.shape[1]
    d_inner = n_heads * headdim
    cw = conv_w.shape[0]
    hpg = n_heads // n_groups
    const = lambda shape: pl.BlockSpec(shape, lambda b, c: (0, 0))
    row_map = lambda b, c: (b * nc + c, 0)
    return pl.pallas_call(
        functools.partial(_ssd_body, chunk=chunk, conv_w=cw, d_inner=d_inner, d_state=d_state,
                          n_groups=n_groups, heads_per_group=hpg, headdim=headdim),
        grid=(batch, nc),
        in_specs=[
            pl.BlockSpec((chunk, conv_dim), row_map),
            pl.BlockSpec((chunk, d_inner), row_map),
            pl.BlockSpec((chunk, LANES), row_map),
            const((cw, conv_dim)), const((1, conv_dim)), const((1, LANES)), const((1, LANES)),
            const((1, d_inner)), const((1, d_inner)), const((3 * LANES, d_inner)),
        ],
        out_specs=[
            pl.BlockSpec((chunk, d_inner), row_map),
            pl.BlockSpec((1, d_inner, d_state), lambda b, c: (b, 0, 0)),
        ],
        out_shape=[
            jax.ShapeDtypeStruct((batch * seq, d_inner), BF16),
            jax.ShapeDtypeStruct((batch, d_inner, d_state), F32),
        ],
        scratch_shapes=[
            pltpu.VMEM((chunk + SUBLANES, conv_dim), F32),
            pltpu.VMEM((n_groups, d_state, hpg * headdim), F32),
            pltpu.VMEM((chunk, d_inner), F32),
        ],
        compiler_params=_params("arbitrary", "arbitrary"),
        name="prompt_ssd",
    )(xbc, z, dt, conv_w, conv_b, dt_bias_p, a_log_p, dskip_x, w_norm, jnp.tile(expand, (3, 1)).astype(BF16))


def _outproj_body(oa_ref, os_ref, x_ref, wa_ref, ws_ref, g_ref, h_ref, u_ref):
    h = x_ref[...] + jnp.dot(oa_ref[...], wa_ref[...], preferred_element_type=F32)
    h = h + jnp.dot(os_ref[...], ws_ref[...], preferred_element_type=F32)
    h_ref[...] = h
    u_ref[...] = (_rms(h, NORM_EPS) * g_ref[...]).astype(u_ref.dtype)


def _outproj(o_attn, o_ssm, x, w_out, g, *, tm_target=512):
    rows, d = x.shape
    tm = _row_tile(rows, tm_target)
    wa, ws = o_attn.shape[1], o_ssm.shape[1]
    assert wa == ws and w_out.shape[0] == wa + ws
    row = lambda w: pl.BlockSpec((tm, w), lambda i: (i, 0))
    return pl.pallas_call(
        _outproj_body,
        grid=(rows // tm,),
        in_specs=[row(wa), row(ws), row(d),
                  pl.BlockSpec((wa, d), lambda i: (0, 0)), pl.BlockSpec((ws, d), lambda i: (1, 0)),
                  pl.BlockSpec((1, d), lambda i: (0, 0))],
        out_specs=[row(d), row(d)],
        out_shape=[jax.ShapeDtypeStruct((rows, d), F32), jax.ShapeDtypeStruct((rows, d), BF16)],
        compiler_params=_params("arbitrary"),
        name="outproj_residual_norm",
    )(o_attn, o_ssm, x, w_out, w_out, g)


def _mlp_body(u_ref, h_ref, wu_ref, wd_ref, o_ref):
    @pl.when(pl.program_id(1) == 0)
    def _():
        o_ref[...] = h_ref[...]

    a = jnp.dot(u_ref[...], wu_ref[...], preferred_element_type=F32)
    a = jnp.square(jnp.maximum(a, 0.0)).astype(BF16)
    o_ref[...] += jnp.dot(a, wd_ref[...], preferred_element_type=F32)


def _mlp(u, h, w_up, w_down, *, tm_target=1024, tf=512):
    rows, d = h.shape
    d_ff = w_up.shape[1]
    tm = _row_tile(rows, tm_target)
    assert d_ff % tf == 0
    return pl.pallas_call(
        _mlp_body,
        grid=(rows // tm, d_ff // tf),
        in_specs=[
            pl.BlockSpec((tm, d), lambda i, f: (i, 0)),
            pl.BlockSpec((tm, d), lambda i, f: (i, 0)),
            pl.BlockSpec((d, tf), lambda i, f: (0, f)),
            pl.BlockSpec((tf, d), lambda i, f: (f, 0)),
        ],
        out_specs=pl.BlockSpec((tm, d), lambda i, f: (i, 0)),
        out_shape=jax.ShapeDtypeStruct((rows, d), F32),
        compiler_params=_params("arbitrary", "arbitrary"),
        name="relu2_mlp",
    )(u, h, w_up, w_down)


def _ple_body(h_ref, p_ref, wp_ref, wpn_ref, wg_ref, gf_ref, o_ref, *, final_norm):
    h = h_ref[...]
    e = jnp.dot(p_ref[...].astype(BF16), wp_ref[...], preferred_element_type=F32)
    e = _rms(e, NORM_EPS) * wpn_ref[...]
    gate = jax.nn.sigmoid(jnp.dot(h.astype(BF16), wg_ref[...], preferred_element_type=F32))
    h = h + e * gate
    if final_norm:
        h = _rms(h, NORM_EPS) * gf_ref[...]
    o_ref[...] = h


def _ple(h, p, w_ple, w_ple_norm, w_gate, g_final, *, final_norm, tm_target=512):
    rows, d = h.shape
    pd = p.shape[1]
    tm = _row_tile(rows, tm_target)
    row = lambda w: pl.BlockSpec((tm, w), lambda i: (i, 0))
    const = lambda shape: pl.BlockSpec(shape, lambda i: (0, 0))
    return pl.pallas_call(
        functools.partial(_ple_body, final_norm=final_norm),
        grid=(rows // tm,),
        in_specs=[row(d), row(pd), const((pd, d)), const((1, d)), const((d, d)), const((1, d))],
        out_specs=row(d),
        out_shape=jax.ShapeDtypeStruct((rows, d), F32),
        compiler_params=_params("arbitrary"),
        name="ple_gate",
    )(h, p, w_ple, w_ple_norm, w_gate, g_final)


def _paged_attn_body(pt_ref, lq1_ref, lk1_ref, lq2_ref, lk2_ref, wsub_ref, q_ref, kn_ref, vn_ref, *rest,
                     pages_per_step, n_heads, dk, page, lam_init):
    del pt_ref
    k_refs = rest[:pages_per_step]
    v_refs = rest[pages_per_step:2 * pages_per_step]
    o_ref, qm_ref, base_ref, m_ref, l_ref, acc_ref = rest[2 * pages_per_step:]
    g = pl.program_id(1)
    n_rows = 2 * n_heads
    n_cols = page * n_heads
    head_bits = int(math.log2(n_heads))
    past_len = pl.num_programs(1) * pages_per_step * page

    rowi = lax.broadcasted_iota(jnp.int32, (n_rows, 1), 0)
    head = jnp.where(rowi >= n_heads, rowi - n_heads, rowi)
    coef = jnp.exp2(-8.0 * (head + 1).astype(F32) / n_heads) * LOG2E

    @pl.when(g == 0)
    def _():
        q = q_ref[0].astype(F32)
        lane = lax.broadcasted_iota(jnp.int32, q.shape, 1)
        qm = jnp.concatenate([jnp.where(lane < dk, q, 0.0), jnp.where(lane >= dk, q, 0.0)], axis=0)
        qm_ref[...] = qm.astype(BF16)
        col = lax.broadcasted_iota(jnp.int32, (n_rows, n_cols), 1)
        own = (col & (n_heads - 1)) == head
        tok = lax.shift_right_logical(col, head_bits)
        dist = (past_len - tok).astype(F32)
        base_ref[...] = jnp.where(own, -coef * dist, -jnp.inf)
        m_ref[...] = jnp.full(m_ref.shape, -jnp.inf, F32)
        l_ref[...] = jnp.zeros(l_ref.shape, F32)
        acc_ref[...] = jnp.zeros(acc_ref.shape, F32)

    qm = qm_ref[...]
    scores, offs, top = [], [], None
    for r in range(pages_per_step):
        off = coef * ((g * pages_per_step + r) * page).astype(F32)
        kp = k_refs[r][...].astype(BF16)
        s = lax.dot_general(qm, kp, (((1,), (1,)), ((), ())), preferred_element_type=F32) + base_ref[...]
        cand = jnp.max(s, axis=-1, keepdims=True) + off
        top = cand if top is None else jnp.maximum(top, cand)
        scores.append(s)
        offs.append(off)
    m_old = m_ref[...]
    m_new = jnp.maximum(m_old, top)
    alpha = jnp.exp2(m_old - m_new)
    l_new = alpha * l_ref[...]
    acc_new = alpha * acc_ref[...]
    for r in range(pages_per_step):
        p = jnp.exp2(scores[r] - (m_new - offs[r]))
        l_new = l_new + jnp.sum(p, axis=-1, keepdims=True)
        acc_new = acc_new + jnp.dot(p.astype(BF16), v_refs[r][...].astype(BF16), preferred_element_type=F32)
    l_ref[...] = l_new
    acc_ref[...] = acc_new
    m_ref[...] = m_new

    @pl.when(g == pl.num_programs(1) - 1)
    def _():
        kn = jnp.concatenate([kn_ref[0], kn_ref[0]], axis=0)
        vn = jnp.concatenate([vn_ref[0], vn_ref[0]], axis=0)
        s_new = jnp.sum(qm_ref[...].astype(F32) * kn, axis=-1, keepdims=True)
        m_old = m_ref[...]
        m_new = jnp.maximum(m_old, s_new)
        p_new = jnp.exp2(s_new - m_new)
        alpha = jnp.exp2(m_old - m_new)
        o_all = (alpha * acc_ref[...] + p_new * vn) / (alpha * l_ref[...] + p_new)
        lam = _lambda_value(lq1_ref, lk1_ref, lq2_ref, lk2_ref, lam_init)
        o = o_all[0:n_heads] - lam * o_all[n_heads:n_rows]
        o_ref[0] = (_rms(o, SUBLN_EPS) * wsub_ref[...] * (1.0 - lam_init)).astype(o_ref.dtype)


def _paged_attention(q_s, k_new, v_new, cache_k, cache_v, page_table, lams, w_subln, *, layer, n_heads, dk, dv,
                     lam_init, pages_per_step=8):
    db, n_pages = page_table.shape
    depth, n_pool, page, _, _ = cache_k.shape
    assert n_pages % pages_per_step == 0 and n_heads & (n_heads - 1) == 0
    rows = page * n_heads
    ck = cache_k.reshape(depth * n_pool * rows, 2 * dk)
    cv = cache_v.reshape(depth * n_pool * rows, dv)
    base = layer * n_pool

    def page_spec(r, width):
        return pl.BlockSpec((rows, width), lambda b, g, pt: (base + pt[b * n_pages + g * pages_per_step + r], 0))

    lam_spec = pl.BlockSpec((1, dk), lambda b, g, pt: (0, 0))
    tok_spec = pl.BlockSpec((1, n_heads, dv), lambda b, g, pt: (b, 0, 0))
    n_rows = 2 * n_heads
    grid_spec = pltpu.PrefetchScalarGridSpec(
        num_scalar_prefetch=1,
        grid=(db, n_pages // pages_per_step),
        in_specs=[lam_spec, lam_spec, lam_spec, lam_spec,
                  pl.BlockSpec((1, dv), lambda b, g, pt: (0, 0)),
                  tok_spec, tok_spec, tok_spec]
                 + [page_spec(r, 2 * dk) for r in range(pages_per_step)]
                 + [page_spec(r, dv) for r in range(pages_per_step)],
        out_specs=tok_spec,
        scratch_shapes=[
            pltpu.VMEM((n_rows, 2 * dk), BF16),
            pltpu.VMEM((n_rows, rows), F32),
            pltpu.VMEM((n_rows, 1), F32),
            pltpu.VMEM((n_rows, 1), F32),
            pltpu.VMEM((n_rows, dv), F32),
        ],
    )
    heads = lambda a: a.reshape(db, n_heads, dv)
    out = pl.pallas_call(
        functools.partial(_paged_attn_body, pages_per_step=pages_per_step, n_heads=n_heads, dk=dk, page=page,
                          lam_init=lam_init),
        grid_spec=grid_spec,
        out_shape=jax.ShapeDtypeStruct((db, n_heads, dv), BF16),
        compiler_params=_params("arbitrary", "arbitrary"),
        name="sample_paged_diff_attention",
    )(page_table.reshape(-1), *lams, w_subln, heads(q_s), heads(k_new), heads(v_new),
      *([ck] * pages_per_step), *([cv] * pages_per_step))
    return out.reshape(db, n_heads * dv)


def _to_column(row_vec, eye):
    n = eye.shape[0]
    return jnp.sum(jnp.where(eye, jnp.broadcast_to(row_vec, (n, n)), 0.0), axis=-1, keepdims=True)


def _to_row(col_vec, eye):
    n = eye.shape[0]
    return jnp.sum(jnp.where(eye, jnp.broadcast_to(col_vec, (n, n)), 0.0), axis=0, keepdims=True)


def _ssm_step_body(xbc_ref, z_ref, dt_ref, sc_ref, st_ref, cw_ref, cb_ref, dtb_ref, alog_ref, dskip_ref, wn_ref,
                   e_ref, o_ref, st_out_ref, *, conv_w, d_inner, d_state, n_groups, heads_per_group, headdim):
    gw = heads_per_group * headdim
    conv = cb_ref[...] + xbc_ref[0] * cw_ref[conv_w - 1:conv_w, :]
    sc = sc_ref[0]
    for j in range(conv_w - 1):
        conv = conv + sc[j:j + 1, :] * cw_ref[j:j + 1, :]
    act = _silu(conv)
    xs = act[:, :d_inner]
    gn = n_groups * d_state

    dtv = _softplus(dt_ref[0] + dtb_ref[...])
    da = dtv * (-jnp.exp(alog_ref[...]))
    expand = e_ref[...]
    dt_x = jnp.dot(dtv, expand, precision=HIGHEST, preferred_element_type=F32)
    decay_x = jnp.exp(jnp.dot(da, expand, precision=HIGHEST, preferred_element_type=F32))
    xd = xs * dt_x

    eye = (lax.broadcasted_iota(jnp.int32, (LANES, LANES), 0)
           == lax.broadcasted_iota(jnp.int32, (LANES, LANES), 1))
    y_tiles = []
    for t in range(d_inner // LANES):
        g = (t * LANES) // gw
        rows = slice(t * LANES, (t + 1) * LANES)
        b_row = act[:, d_inner + g * d_state:d_inner + (g + 1) * d_state]
        c_row = act[:, d_inner + gn + g * d_state:d_inner + gn + (g + 1) * d_state]
        st = st_ref[0, rows, :] * _to_column(decay_x[:, rows], eye) + _to_column(xd[:, rows], eye) * b_row
        st_out_ref[0, rows, :] = st
        y_tiles.append(_to_row(jnp.sum(st * c_row, axis=-1, keepdims=True), eye))
    y = jnp.concatenate(y_tiles, axis=-1) + dskip_ref[...] * xs
    gated = y * _silu(z_ref[0])
    wn = wn_ref[...]
    outs = []
    for g in range(n_groups):
        gs = slice(g * gw, (g + 1) * gw)
        outs.append(_rms(gated[:, gs], NORM_EPS) * wn[:, gs])
    o_ref[0] = jnp.concatenate(outs, axis=-1).astype(o_ref.dtype)


def _sample_ssm(xbc, z, dt, state_conv, state_ssm, conv_w, conv_b, dt_bias_p, a_log_p, dskip_x, w_norm, expand, *,
                d_state, n_groups, n_heads, headdim):
    db, conv_dim = xbc.shape
    d_inner = n_heads * headdim
    cw = conv_w.shape[0]
    per_b = lambda *tail: pl.BlockSpec((1,) + tail, lambda b: (b,) + (0,) * len(tail))
    const = lambda shape: pl.BlockSpec(shape, lambda b: (0, 0))
    o, st = pl.pallas_call(
        functools.partial(_ssm_step_body, conv_w=cw, d_inner=d_inner, d_state=d_state, n_groups=n_groups,
                          heads_per_group=n_heads // n_groups, headdim=headdim),
        grid=(db,),
        in_specs=[
            per_b(1, conv_dim), per_b(1, d_inner), per_b(1, LANES), per_b(cw - 1, conv_dim),
            per_b(d_inner, d_state),
            const((cw, conv_dim)), const((1, conv_dim)), const((1, LANES)), const((1, LANES)),
            const((1, d_inner)), const((1, d_inner)), const((LANES, d_inner)),
        ],
        out_specs=[per_b(1, d_inner), per_b(d_inner, d_state)],
        out_shape=[jax.ShapeDtypeStruct((db, 1, d_inner), BF16),
                   jax.ShapeDtypeStruct((db, d_inner, d_state), F32)],
        compiler_params=_params("arbitrary"),
        name="sample_ssm_step",
    )(xbc.reshape(db, 1, conv_dim), z.reshape(db, 1, d_inner), dt.reshape(db, 1, LANES), state_conv,
      state_ssm.reshape(db, d_inner, d_state), conv_w, conv_b, dt_bias_p, a_log_p, dskip_x, w_norm, expand)
    return o.reshape(db, d_inner), st


def _pad_lanes(v):
    return jnp.pad(v.astype(F32), (0, LANES - v.shape[0])).reshape(1, LANES)


def kernel(x_prompt, x_sample, p_prompt, p_sample, cache_k, cache_v, page_table, state_ssm, state_conv, g_mix, w_in, lambda_q1, lambda_k1, lambda_q2, lambda_k2, w_subln, conv_w, conv_b, dt_bias, a_log, d_skip, w_ssm_norm, w_out, g_ffn, w_up, w_down, w_ple, w_ple_norm, w_ple_gate, g_final):
    batch, seq, d_model = x_prompt.shape
    db, dec_seq, _ = x_sample.shape
    assert dec_seq == 1, "the sample kernels handle one new token per sequence"
    depth = w_in.shape[0]
    n_heads_a = cache_k.shape[3]
    dv = cache_v.shape[4]
    dk = lambda_q1.shape[1]
    assert cache_k.shape[4] == 2 * dk == dv == LANES
    aw = n_heads_a * dv
    n_heads_s, headdim, d_state = state_ssm.shape[2:]
    d_inner = n_heads_s * headdim
    conv_dim = state_conv.shape[3]
    n_groups = (conv_dim - d_inner) // (2 * d_state)
    assert n_heads_s <= LANES and LANES % headdim == 0 and d_state == LANES
    n_main = 3 * aw + d_inner + conv_dim
    q_scale = float(dk) ** -0.5 * LOG2E
    expand = _head_expand(n_heads_s, headdim)

    hp = x_prompt.reshape(batch * seq, d_model)
    hs = x_sample.reshape(db, d_model)
    outs = {name: [] for name in ("kp", "vp", "sp", "cp", "ks", "vs", "ss", "cs")}
    for i in range(depth):
        lam_init = 0.8 - 0.6 * math.exp(-0.3 * i)
        last = i == depth - 1
        w_main = w_in[i].astype(BF16)
        w_dt = jnp.pad(w_in[i][:, n_main:].astype(BF16), ((0, 0), (0, LANES - n_heads_s)))
        w_out_i = w_out[i].astype(BF16)
        w_up_i, w_down_i = w_up[i].astype(BF16), w_down[i].astype(BF16)
        w_ple_i, w_gate_i = w_ple[i].astype(BF16), w_ple_gate[i].astype(BF16)
        g_mix_i, g_ffn_i = g_mix[i].reshape(1, -1), g_ffn[i].reshape(1, -1)
        lams = tuple(v[i].reshape(1, dk) for v in (lambda_q1, lambda_k1, lambda_q2, lambda_k2))
        w_subln_i = w_subln[i].reshape(1, dv)
        conv_b_i = conv_b[i].reshape(1, conv_dim)
        dt_bias_p, a_log_p = _pad_lanes(dt_bias[i]), _pad_lanes(a_log[i])
        dskip_x = jnp.repeat(d_skip[i].astype(F32), headdim).reshape(1, d_inner)
        w_norm_i = w_ssm_norm[i].reshape(1, d_inner)
        w_ple_norm_i = w_ple_norm[i].reshape(1, d_model)
        g_final_r = g_final.reshape(1, d_model)
        proj = functools.partial(_inproj, g=g_mix_i, w_main=w_main, w_dt=w_dt, widths=(aw, d_inner, conv_dim),
                                 q_scale=q_scale)
        ssm_kw = dict(d_state=d_state, n_groups=n_groups, n_heads=n_heads_s, headdim=headdim)

        def channel(h, o_attn, o_ssm, p):
            h1, u = _outproj(o_attn, o_ssm, h, w_out_i, g_ffn_i)
            h2 = _mlp(u, h1, w_up_i, w_down_i)
            return _ple(h2, p, w_ple_i, w_ple_norm_i, w_gate_i, g_final_r, final_norm=last)

        q_bf, k32, v32, k_bf, v_bf, z, xbc, dt = proj(hp)
        o_attn = _prompt_attention(q_bf, k_bf, v_bf, lams, w_subln_i, batch=batch, seq=seq, n_heads=n_heads_a,
                                   dk=dk, dv=dv, lam_init=lam_init)
        o_ssm, st = _prompt_ssd(xbc, z, dt, conv_w[i], conv_b_i, dt_bias_p, a_log_p, dskip_x, w_norm_i, expand,
                                batch=batch, seq=seq, **ssm_kw)
        hp = channel(hp, o_attn, o_ssm, p_prompt[i].reshape(batch * seq, -1))
        outs["kp"].append(k32.reshape(batch, seq, n_heads_a, 2 * dk))
        outs["vp"].append(v32.reshape(batch, seq, n_heads_a, dv))
        outs["sp"].append(st.reshape(batch, n_heads_s, headdim, d_state))
        cw = conv_w.shape[1]
        outs["cp"].append(xbc.reshape(batch, seq, conv_dim)[:, seq - (cw - 1):])

        q_bf, k32, v32, _, _, z, xbc, dt = proj(hs)
        o_attn = _paged_attention(q_bf, k32, v32, cache_k, cache_v, page_table, lams, w_subln_i, layer=i,
                                  n_heads=n_heads_a, dk=dk, dv=dv, lam_init=lam_init)
        o_ssm, st = _sample_ssm(xbc, z, dt, state_conv[i], state_ssm[i], conv_w[i], conv_b_i, dt_bias_p, a_log_p,
                                dskip_x, w_norm_i, expand, **ssm_kw)
        hs = channel(hs, o_attn, o_ssm, p_sample[i].reshape(db, -1))
        outs["ks"].append(k32.reshape(db, 1, n_heads_a, 2 * dk))
        outs["vs"].append(v32.reshape(db, 1, n_heads_a, dv))
        outs["ss"].append(st.reshape(db, n_heads_s, headdim, d_state))
        outs["cs"].append(jnp.concatenate([state_conv[i][:, 1:], xbc[:, None, :]], axis=1))

    stack = lambda name: jnp.stack(outs[name])
    return (hp.reshape(batch, seq, d_model), hs.reshape(db, 1, d_model), stack("kp"), stack("vp"), stack("sp"),
            stack("cp"), stack("ks"), stack("vs"), stack("ss"), stack("cs"))
```

```python
import functools
import math

import jax
import jax.numpy as jnp
import numpy as np
from jax import lax
from jax.experimental import pallas as pl
from jax.experimental.pallas import tpu as pltpu

F32 = jnp.float32
BF16 = jnp.bfloat16
NORM_EPS = 1e-6
SUBLN_EPS = 1e-5
HIGHEST = lax.Precision.HIGHEST
LOG2E = math.log2(math.e)

LANES = 128
SUBLANES = 8
VMEM_LIMIT_BYTES = 56 * 1024 * 1024


def _params(*semantics):
    return pltpu.CompilerParams(dimension_semantics=semantics, vmem_limit_bytes=VMEM_LIMIT_BYTES)


def _rms(x, eps):
    return x * lax.rsqrt(jnp.mean(x * x, axis=-1, keepdims=True) + eps)


def _silu(x):
    h = 0.5 * x
    return h + h * jnp.tanh(h)


def _softplus(x):
    return jnp.maximum(x, 0.0) + jnp.log1p(jnp.exp(-jnp.abs(x)))


def _row_tile(rows, target):
    t = min(rows, target)
    assert rows % t == 0, (rows, t)
    return t


def _inproj_body(x_ref, g_ref, w_ref, wdt_ref, q_ref, k32_ref, v32_ref, kbf_ref, vbf_ref, z_ref,
                 xbc_ref, dt_ref, u_ref, *, bounds, q_scale):
    j = pl.program_id(1)

    @pl.when(j == 0)
    def _():
        u = _rms(x_ref[...], NORM_EPS) * g_ref[...]
        u_ref[...] = u.astype(BF16)
        dt_ref[...] = jnp.dot(u_ref[...], wdt_ref[...], preferred_element_type=F32)

    def tile():
        return jnp.dot(u_ref[...], w_ref[...], preferred_element_type=F32)

    b_q, b_k, b_v, b_z = bounds

    @pl.when(j < b_q)
    def _():
        q_ref[...] = (tile() * q_scale).astype(BF16)

    @pl.when((j >= b_q) & (j < b_k))
    def _():
        res = tile()
        k32_ref[...] = res
        kbf_ref[...] = res.astype(BF16)

    @pl.when((j >= b_k) & (j < b_v))
    def _():
        res = tile()
        v32_ref[...] = res
        vbf_ref[...] = res.astype(BF16)

    @pl.when((j >= b_v) & (j < b_z))
    def _():
        z_ref[...] = tile()

    @pl.when(j >= b_z)
    def _():
        xbc_ref[...] = tile()


def _inproj(x, g, w_main, w_dt, *, widths, q_scale, tm_target=1024, tn=512):
    rows, d = x.shape
    tm = _row_tile(rows, tm_target)
    aw, d_inner, conv_dim = widths
    seg = [aw, aw, aw, d_inner, conv_dim]
    assert all(s % tn == 0 for s in seg)
    tiles = [s // tn for s in seg]
    starts = [sum(tiles[:i]) for i in range(len(tiles))]
    n_tiles = sum(tiles)
    bounds = tuple(starts[i] + tiles[i] for i in range(4))

    def seg_map(s):
        lo, n = starts[s], tiles[s]
        return lambda i, j: (i, jnp.clip(j - lo, 0, n - 1))

    def out(s, dtype):
        return jax.ShapeDtypeStruct((rows, seg[s]), dtype), pl.BlockSpec((tm, tn), seg_map(s))

    outs = [out(0, BF16), out(1, F32), out(2, F32), out(1, BF16), out(2, BF16), out(3, F32), out(4, F32)]
    out_shape = [o[0] for o in outs] + [jax.ShapeDtypeStruct((rows, LANES), F32)]
    out_specs = [o[1] for o in outs] + [pl.BlockSpec((tm, LANES), lambda i, j: (i, 0))]
    return pl.pallas_call(
        functools.partial(_inproj_body, bounds=bounds, q_scale=q_scale),
        grid=(rows // tm, n_tiles),
        in_specs=[
            pl.BlockSpec((tm, d), lambda i, j: (i, 0)),
            pl.BlockSpec((1, d), lambda i, j: (0, 0)),
            pl.BlockSpec((d, tn), lambda i, j: (0, j)),
            pl.BlockSpec((d, LANES), lambda i, j: (0, 0)),
        ],
        out_specs=out_specs,
        out_shape=out_shape,
        scratch_shapes=[pltpu.VMEM((tm, d), BF16)],
        compiler_params=_params("arbitrary", "arbitrary"),
        name="norm_inproj",
    )(x, g, w_main, w_dt)


def _lambda_value(lq1_ref, lk1_ref, lq2_ref, lk2_ref, lam_init):
    s1 = jnp.sum(lq1_ref[...] * lk1_ref[...], axis=-1, keepdims=True)
    s2 = jnp.sum(lq2_ref[...] * lk2_ref[...], axis=-1, keepdims=True)
    return jnp.exp(s1) - jnp.exp(s2) + lam_init


def _attn_body(coef_ref, lq1_ref, lk1_ref, lq2_ref, lk2_ref, wsub_ref, qf_ref, kf_ref, q_ref, k_ref, v_ref, o_ref,
               vt_ref, q2_ref, s_ref, m_ref, acc_ref, *, tq, dk, dv, heads, lam_init):
    hg = pl.program_id(1)
    qi = pl.program_id(2)
    cols2 = 2 * tq
    n_kb, tk = vt_ref.shape[1], vt_ref.shape[3]
    assert tk == tq
    dvx = vt_ref.shape[2]

    @pl.when(qi == 0)
    def _():
        for hb in range(heads):
            for kb in range(n_kb):
                for part in range(tk // LANES):
                    rows = slice(kb * tk + part * LANES, kb * tk + (part + 1) * LANES)
                    vt = v_ref[rows, hb * dv:(hb + 1) * dv].astype(F32).T
                    vt_ref[hb, kb, 0:dv, part * LANES:(part + 1) * LANES] = vt.astype(BF16)
                vt_ref[hb, kb, dv:dvx, :] = jnp.ones((dvx - dv, tk), BF16)
            q2_ref[hb, 0:2 * dk, :] = jnp.zeros((2 * dk, cols2), BF16)
            q2_ref[hb, 2 * dk:, 0:tq] = qf_ref[hb]
            q2_ref[hb, 2 * dk:, tq:cols2] = qf_ref[hb]

    for hb in range(heads):
        qt = q_ref[:, hb * 2 * dk:(hb + 1) * 2 * dk].astype(F32).T.astype(BF16)
        q2_ref[hb, 0:dk, 0:tq] = qt[0:dk]
        q2_ref[hb, dk:2 * dk, tq:cols2] = qt[dk:2 * dk]
    m_ref[...] = jnp.full(m_ref.shape, -jnp.inf, F32)
    acc_ref[...] = jnp.zeros(acc_ref.shape, F32)

    def score(hb, kb, slot):
        start = pl.multiple_of(kb * tk, tk)
        kx = jnp.concatenate([k_ref[pl.ds(start, tk), hb * 2 * dk:(hb + 1) * 2 * dk], kf_ref[hb]], axis=-1)
        s_ref[hb, slot] = jnp.dot(kx, q2_ref[hb], preferred_element_type=F32)

    def accumulate(hb, kb, slot, masked):
        s = s_ref[hb, slot]
        off = -coef_ref[hg * heads + hb] * ((qi - kb) * tq).astype(F32)
        if masked:
            key = lax.broadcasted_iota(jnp.int32, (tk, cols2), 0)
            col = lax.broadcasted_iota(jnp.int32, (tk, cols2), 1)
            s = jnp.where(jnp.where(col >= tq, col - tq, col) >= key, s, -jnp.inf)
        m_old = m_ref[hb]
        m_new = jnp.maximum(m_old, jnp.max(s, axis=0, keepdims=True) + off)
        p = jnp.exp2(s - (m_new - off)).astype(BF16)
        alpha = jnp.exp2(m_old - m_new)
        acc_ref[hb] = alpha * acc_ref[hb] + jnp.dot(vt_ref[hb, kb], p, preferred_element_type=F32)
        m_ref[hb] = m_new

    def stage(kb_next, kb, slot, masked=False):
        for hb in range(heads):
            if kb_next is not None:
                score(hb, kb_next, 1 - slot)
            accumulate(hb, kb, slot, masked)

    def pair(j, carry):
        stage(2 * j + 1, 2 * j, 0)
        stage(2 * j + 2, 2 * j + 1, 1)
        return carry

    for hb in range(heads):
        score(hb, 0, 0)
    lax.fori_loop(0, qi // 2, pair, 0)

    @pl.when(qi % 2 == 0)
    def _():
        stage(None, qi, 0, masked=True)

    @pl.when(qi % 2 == 1)
    def _():
        stage(qi, qi - 1, 0)
        stage(None, qi, 1, masked=True)

    lam = _lambda_value(lq1_ref, lk1_ref, lq2_ref, lk2_ref, lam_init)
    for hb in range(heads):
        acc = acc_ref[hb]
        o = acc[0:dv] / acc[dv:dv + 1]
        o = o[:, 0:tq] - lam * o[:, tq:cols2]
        o = o * lax.rsqrt(jnp.mean(o * o, axis=0, keepdims=True) + SUBLN_EPS)
        o = o * wsub_ref[...] * (1.0 - lam_init)
        o_ref[:, hb * dv:(hb + 1) * dv] = o.T.astype(o_ref.dtype)


def _bf16_split3(v):
    hi = v.astype(BF16)
    r1 = v - hi.astype(np.float32)
    mid = r1.astype(BF16)
    lo = (r1 - mid.astype(np.float32)).astype(BF16)
    return [hi, mid, lo]


def _alibi_lanes(n_heads, t):
    slopes = np.float32(2.0) ** (np.float32(-8.0) * np.arange(1, n_heads + 1, dtype=np.float32) / np.float32(n_heads))
    coef = (slopes * np.float32(LOG2E)).astype(np.float32)
    cpos = coef[:, None] * np.arange(t, dtype=np.float32)[None, :]
    ones = [np.ones((n_heads, t), BF16)] * 3
    pad = [np.zeros((n_heads, t), BF16)] * (LANES - 6)
    q_lanes = np.stack(_bf16_split3(-cpos) + ones + pad, axis=1)
    k_lanes = np.stack(ones + _bf16_split3(cpos) + pad, axis=-1)
    return jnp.asarray(coef), jnp.asarray(q_lanes), jnp.asarray(k_lanes)


def _prompt_attention(q_bf, k_bf, v_bf, lams, w_subln, *, batch, seq, n_heads, dk, dv, lam_init, tq=256, heads=4):
    tq = _row_tile(seq, tq)
    nq = seq // tq
    assert n_heads % heads == 0
    coef, q_lanes, k_lanes = _alibi_lanes(n_heads, tq)
    ones_rows = 2 * SUBLANES
    lam_spec = pl.BlockSpec((1, dk), lambda b, h, i: (0, 0))
    qlanes_spec = pl.BlockSpec((heads, LANES, tq), lambda b, h, i: (h, 0, 0))
    klanes_spec = pl.BlockSpec((heads, tq, LANES), lambda b, h, i: (h, 0, 0))
    return pl.pallas_call(
        functools.partial(_attn_body, tq=tq, dk=dk, dv=dv, heads=heads, lam_init=lam_init),
        grid=(batch, n_heads // heads, nq),
        in_specs=[
            pl.BlockSpec(memory_space=pltpu.SMEM),
            lam_spec, lam_spec, lam_spec, lam_spec,
            pl.BlockSpec((dv, 1), lambda b, h, i: (0, 0)),
            qlanes_spec, klanes_spec,
            pl.BlockSpec((tq, heads * 2 * dk), lambda b, h, i: (b * nq + i, h)),
            pl.BlockSpec((seq, heads * 2 * dk), lambda b, h, i: (b, h)),
            pl.BlockSpec((seq, heads * dv), lambda b, h, i: (b, h)),
        ],
        out_specs=pl.BlockSpec((tq, heads * dv), lambda b, h, i: (b * nq + i, h)),
        out_shape=jax.ShapeDtypeStruct((batch * seq, n_heads * dv), BF16),
        scratch_shapes=[
            pltpu.VMEM((heads, nq, dv + ones_rows, tq), BF16),
            pltpu.VMEM((heads, 2 * dk + LANES, 2 * tq), BF16),
            pltpu.VMEM((heads, 2, tq, 2 * tq), F32),
            pltpu.VMEM((heads, 1, 2 * tq), F32),
            pltpu.VMEM((heads, dv + ones_rows, 2 * tq), F32),
        ],
        compiler_params=_params("arbitrary", "arbitrary", "arbitrary"),
        name="prompt_diff_attention",
    )(coef, *lams, w_subln.reshape(dv, 1), q_lanes, k_lanes, q_bf, k_bf, v_bf)


def _ssd_body(xbc_ref, z_ref, dt_ref, cw_ref, cb_ref, dtb_ref, alog_ref, dskip_ref, wn_ref, e_ref,
              o_ref, st_out_ref, ext_ref, st_ref, y_ref, *, chunk, conv_w, d_inner, d_state, n_groups,
              heads_per_group, headdim):
    c = pl.program_id(1)
    n_chunks = pl.num_programs(1)
    pad = SUBLANES
    gw = heads_per_group * headdim

    @pl.when(c == 0)
    def _():
        ext_ref[0:pad, :] = jnp.zeros((pad, ext_ref.shape[1]), F32)
        st_ref[...] = jnp.zeros(st_ref.shape, F32)

    ext_ref[pad:pad + chunk, :] = xbc_ref[...]
    ext = ext_ref[...]
    conv = cb_ref[...] + ext[pad:pad + chunk] * cw_ref[conv_w - 1:conv_w, :]
    for j in range(conv_w - 1):
        shifted = pltpu.roll(ext, conv_w - 1 - j, axis=0)
        conv = conv + shifted[pad:pad + chunk] * cw_ref[j:j + 1, :]
    ext_ref[0:pad, :] = ext[chunk:chunk + pad]
    act = _silu(conv)

    xs = act[:, :d_inner]
    gn = n_groups * d_state
    bm = act[:, d_inner:d_inner + gn]
    cm = act[:, d_inner + gn:d_inner + 2 * gn]

    dtv = _softplus(dt_ref[...] + dtb_ref[...])
    da = dtv * (-jnp.exp(alog_ref[...]))
    r = lax.broadcasted_iota(jnp.int32, (chunk, chunk), 0)
    s = lax.broadcasted_iota(jnp.int32, (chunk, chunk), 1)
    causal = r >= s
    tri = causal.astype(F32).astype(BF16)
    a_cs = jnp.dot(jnp.concatenate([tri, tri, tri], axis=1), jnp.concatenate(_split3(da), axis=0),
                   preferred_element_type=F32)
    a_cs_t = a_cs.T

    both = jnp.concatenate([dtv, a_cs], axis=0)
    both_x = jnp.dot(jnp.concatenate(_split3(both), axis=1), e_ref[...], preferred_element_type=F32)
    dt_x = both_x[0:chunk]
    acs_x = both_x[chunk:2 * chunk]
    alast_x = acs_x[chunk - 1:chunk, :]
    xd = xs * dt_x
    xd_b = xd.astype(BF16)
    xdd_b = (xd * jnp.exp(alast_x - acs_x)).astype(BF16)
    eacs_x = jnp.exp(acs_x)
    chunk_decay_x = jnp.exp(alast_x)

    lane = lax.broadcasted_iota(jnp.int32, (chunk, LANES), 1)
    heads_per_tile = LANES // headdim
    for g in range(n_groups):
        bm_g = bm[:, g * d_state:(g + 1) * d_state]
        cm_b = cm[:, g * d_state:(g + 1) * d_state].astype(BF16)
        cb = lax.dot_general(cm_b, bm_g.astype(BF16), (((1,), (1,)), ((), ())), preferred_element_type=F32)
        gs = slice(g * gw, (g + 1) * gw)
        st_g = st_ref[g]
        y_off = jnp.dot(cm_b, st_g.astype(BF16), preferred_element_type=F32) * eacs_x[:, gs]
        new = jnp.dot(bm_g.T.astype(BF16), xdd_b[:, gs], preferred_element_type=F32)
        st_ref[g] = st_g * chunk_decay_x[:, gs] + new
        for t in range(gw // LANES):
            lo = g * gw + t * LANES
            xd_t = xd_b[:, lo:lo + LANES]
            y_t = y_off[:, t * LANES:(t + 1) * LANES]
            for k in range(heads_per_tile):
                hh = g * heads_per_group + t * heads_per_tile + k
                seg = a_cs[:, hh:hh + 1] - a_cs_t[hh:hh + 1, :]
                lmat = jnp.exp(jnp.where(causal, seg, -jnp.inf))
                in_head = (lane >= k * headdim) & (lane < (k + 1) * headdim)
                xd_h = jnp.where(in_head, xd_t, jnp.zeros_like(xd_t))
                y_t = y_t + jnp.dot((cb * lmat).astype(BF16), xd_h, preferred_element_type=F32)
            y_ref[:, lo:lo + LANES] = y_t

    y = y_ref[...] + dskip_ref[...] * xs
    gated = y * _silu(z_ref[...])
    wn = wn_ref[...]
    for g in range(n_groups):
        gs = slice(g * gw, (g + 1) * gw)
        o_ref[:, gs] = (_rms(gated[:, gs], NORM_EPS) * wn[:, gs]).astype(o_ref.dtype)

    @pl.when(c == n_chunks - 1)
    def _():
        for g in range(n_groups):
            st_out_ref[0, g * gw:(g + 1) * gw, :] = st_ref[g].T


def _split3(x):
    hi = x.astype(BF16)
    r1 = x - hi.astype(F32)
    mid = r1.astype(BF16)
    lo = (r1 - mid.astype(F32)).astype(BF16)
    return [hi, mid, lo]


def _head_expand(n_heads, headdim):
    head_of_lane = jnp.arange(n_heads * headdim) // headdim
    return (jnp.arange(LANES)[:, None] == head_of_lane[None, :]).astype(F32)


def _prompt_ssd(xbc, z, dt, conv_w, conv_b, dt_bias_p, a_log_p, dskip_x, w_norm, expand, *, batch, seq,
                d_state, n_groups, n_heads, headdim, chunk=128):
    chunk = _row_tile(seq, chunk)
    nc = seq // chunk
    conv_dim = xbc.shape[1]
    d_inner = n_heads * headdim
    cw = conv_w.shape[0]
    hpg = n_heads // n_groups
    const = lambda shape: pl.BlockSpec(shape, lambda b, c: (0, 0))
    row_map = lambda b, c: (b * nc + c, 0)
    return pl.pallas_call(
        functools.partial(_ssd_body, chunk=chunk, conv_w=cw, d_inner=d_inner, d_state=d_state,
                          n_groups=n_groups, heads_per_group=hpg, headdim=headdim),
        grid=(batch, nc),
        in_specs=[
            pl.BlockSpec((chunk, conv_dim), row_map),
            pl.BlockSpec((chunk, d_inner), row_map),
            pl.BlockSpec((chunk, LANES), row_map),
            const((cw, conv_dim)), const((1, conv_dim)), const((1, LANES)), const((1, LANES)),
            const((1, d_inner)), const((1, d_inner)), const((3 * LANES, d_inner)),
        ],
        out_specs=[
            pl.BlockSpec((chunk, d_inner), row_map),
            pl.BlockSpec((1, d_inner, d_state), lambda b, c: (b, 0, 0)),
        ],
        out_shape=[
            jax.ShapeDtypeStruct((batch * seq, d_inner), BF16),
            jax.ShapeDtypeStruct((batch, d_inner, d_state), F32),
        ],
        scratch_shapes=[
            pltpu.VMEM((chunk + SUBLANES, conv_dim), F32),
            pltpu.VMEM((n_groups, d_state, hpg * headdim), F32),
            pltpu.VMEM((chunk, d_inner), F32),
        ],
        compiler_params=_params("arbitrary", "arbitrary"),
        name="prompt_ssd",
    )(xbc, z, dt, conv_w, conv_b, dt_bias_p, a_log_p, dskip_x, w_norm, jnp.tile(expand, (3, 1)).astype(BF16))


def _outproj_body(oa_ref, os_ref, x_ref, wa_ref, ws_ref, g_ref, h_ref, u_ref):
    h = x_ref[...] + jnp.dot(oa_ref[...], wa_ref[...], preferred_element_type=F32)
    h = h + jnp.dot(os_ref[...], ws_ref[...], preferred_element_type=F32)
    h_ref[...] = h
    u_ref[...] = (_rms(h, NORM_EPS) * g_ref[...]).astype(u_ref.dtype)


def _outproj(o_attn, o_ssm, x, w_out, g, *, tm_target=512):
    rows, d = x.shape
    tm = _row_tile(rows, tm_target)
    wa, ws = o_attn.shape[1], o_ssm.shape[1]
    assert wa == ws and w_out.shape[0] == wa + ws
    row = lambda w: pl.BlockSpec((tm, w), lambda i: (i, 0))
    return pl.pallas_call(
        _outproj_body,
        grid=(rows // tm,),
        in_specs=[row(wa), row(ws), row(d),
                  pl.BlockSpec((wa, d), lambda i: (0, 0)), pl.BlockSpec((ws, d), lambda i: (1, 0)),
                  pl.BlockSpec((1, d), lambda i: (0, 0))],
        out_specs=[row(d), row(d)],
        out_shape=[jax.ShapeDtypeStruct((rows, d), F32), jax.ShapeDtypeStruct((rows, d), BF16)],
        compiler_params=_params("arbitrary"),
        name="outproj_residual_norm",
    )(o_attn, o_ssm, x, w_out, w_out, g)


def _mlp_accumulate(u_ref, wu_ref, wd_ref, o_ref):
    a = jnp.dot(u_ref[...], wu_ref[...], preferred_element_type=F32)
    a = jnp.square(jnp.maximum(a, 0.0)).astype(BF16)
    o_ref[...] += jnp.dot(a, wd_ref[...], preferred_element_type=F32)


def _mlp_step(u_ref, wu_ref, wd_ref, o_ref):
    @pl.when(pl.program_id(1) == 0)
    def _():
        o_ref[...] = jnp.zeros(o_ref.shape, F32)

    _mlp_accumulate(u_ref, wu_ref, wd_ref, o_ref)


def _mlp(u, w_up, w_down, *, tm_target=1024, tf=512):
    rows, d = u.shape
    d_ff = w_up.shape[1]
    tm = _row_tile(rows, tm_target)
    assert d_ff % tf == 0
    return pl.pallas_call(
        _mlp_step,
        grid=(rows // tm, d_ff // tf),
        in_specs=[
            pl.BlockSpec((tm, d), lambda i, f: (i, 0)),
            pl.BlockSpec((d, tf), lambda i, f: (0, f)),
            pl.BlockSpec((tf, d), lambda i, f: (f, 0)),
        ],
        out_specs=pl.BlockSpec((tm, d), lambda i, f: (i, 0)),
        out_shape=jax.ShapeDtypeStruct((rows, d), F32),
        compiler_params=_params("arbitrary", "arbitrary"),
        name="relu2_mlp",
    )(u, w_up, w_down)


def _ple_body(h_ref, m_ref, p_ref, wp_ref, wpn_ref, wg_ref, gf_ref, o_ref, *, final_norm):
    h = h_ref[...] + m_ref[...]
    e = jnp.dot(p_ref[...].astype(BF16), wp_ref[...], preferred_element_type=F32)
    e = _rms(e, NORM_EPS) * wpn_ref[...]
    gate = jax.nn.sigmoid(jnp.dot(h.astype(BF16), wg_ref[...], preferred_element_type=F32))
    h = h + e * gate
    if final_norm:
        h = _rms(h, NORM_EPS) * gf_ref[...]
    o_ref[...] = h


def _ple(h, mlp_out, p, w_ple, w_ple_norm, w_gate, g_final, *, final_norm, tm_target=512):
    rows, d = h.shape
    pd = p.shape[1]
    tm = _row_tile(rows, tm_target)
    row = lambda w: pl.BlockSpec((tm, w), lambda i: (i, 0))
    const = lambda shape: pl.BlockSpec(shape, lambda i: (0, 0))
    return pl.pallas_call(
        functools.partial(_ple_body, final_norm=final_norm),
        grid=(rows // tm,),
        in_specs=[row(d), row(d), row(pd), const((pd, d)), const((1, d)), const((d, d)), const((1, d))],
        out_specs=row(d),
        out_shape=jax.ShapeDtypeStruct((rows, d), F32),
        compiler_params=_params("arbitrary"),
        name="ple_gate",
    )(h, mlp_out, p, w_ple, w_ple_norm, w_gate, g_final)


def _mlp_paged_body(pt_ref, u_ref, wu_ref, wd_ref, lq1_ref, lk1_ref, lq2_ref, lk2_ref, wsub_ref, q_ref, kn_ref, vn_ref,
                    *rest, pages_per_step, n_heads, dk, page, lam_init):
    del pt_ref
    k_refs = rest[:pages_per_step]
    v_refs = rest[pages_per_step:2 * pages_per_step]
    o_ref, oa_ref, qm_ref, base_ref, m_ref, l_ref, acc_ref = rest[2 * pages_per_step:]
    g = pl.program_id(1)
    n_rows = 2 * n_heads
    n_cols = page * n_heads
    head_bits = int(math.log2(n_heads))
    past_len = pl.num_programs(1) * pages_per_step * page

    rowi = lax.broadcasted_iota(jnp.int32, (n_rows, 1), 0)
    head = rowi & (n_heads - 1)
    coef = jnp.exp2(-8.0 * (head + 1).astype(F32) / n_heads) * LOG2E

    @pl.when(g == 0)
    def _():
        o_ref[...] = jnp.zeros(o_ref.shape, F32)
        q = q_ref[0].astype(F32)
        lane = lax.broadcasted_iota(jnp.int32, q.shape, 1)
        qm = jnp.concatenate([jnp.where(lane < dk, q, 0.0), jnp.where(lane >= dk, q, 0.0)], axis=0)
        qm_ref[...] = qm.astype(BF16)
        col = lax.broadcasted_iota(jnp.int32, (n_rows, n_cols), 1)
        own = (col & (n_heads - 1)) == head
        tok = lax.shift_right_logical(col, head_bits)
        dist = (past_len - tok).astype(F32)
        base_ref[...] = jnp.where(own, -coef * dist, -jnp.inf)
        m_ref[...] = jnp.full(m_ref.shape, -jnp.inf, F32)
        l_ref[...] = jnp.zeros(l_ref.shape, F32)
        acc_ref[...] = jnp.zeros(acc_ref.shape, F32)

    _mlp_accumulate(u_ref, wu_ref, wd_ref, o_ref)

    qm = qm_ref[...]
    scores, offs, top = [], [], None
    for r in range(pages_per_step):
        off = coef * ((g * pages_per_step + r) * page).astype(F32)
        kp = k_refs[r][...].astype(BF16)
        s = lax.dot_general(qm, kp, (((1,), (1,)), ((), ())), preferred_element_type=F32) + base_ref[...]
        cand = jnp.max(s, axis=-1, keepdims=True) + off
        top = cand if top is None else jnp.maximum(top, cand)
        scores.append(s)
        offs.append(off)
    m_old = m_ref[...]
    m_new = jnp.maximum(m_old, top)
    alpha = jnp.exp2(m_old - m_new)
    l_new = alpha * l_ref[...]
    acc_new = alpha * acc_ref[...]
    for r in range(pages_per_step):
        p = jnp.exp2(scores[r] - (m_new - offs[r]))
        l_new = l_new + jnp.sum(p, axis=-1, keepdims=True)
        acc_new = acc_new + jnp.dot(p.astype(BF16), v_refs[r][...].astype(BF16), preferred_element_type=F32)
    l_ref[...] = l_new
    acc_ref[...] = acc_new
    m_ref[...] = m_new

    @pl.when(g == pl.num_programs(1) - 1)
    def _():
        kn = jnp.concatenate([kn_ref[0], kn_ref[0]], axis=0)
        vn = jnp.concatenate([vn_ref[0], vn_ref[0]], axis=0)
        s_new = jnp.sum(qm_ref[...].astype(F32) * kn, axis=-1, keepdims=True)
        m_old = m_ref[...]
        m_new = jnp.maximum(m_old, s_new)
        p_new = jnp.exp2(s_new - m_new)
        alpha = jnp.exp2(m_old - m_new)
        o_all = (alpha * acc_ref[...] + p_new * vn) / (alpha * l_ref[...] + p_new)
        lam = _lambda_value(lq1_ref, lk1_ref, lq2_ref, lk2_ref, lam_init)
        o = o_all[0:n_heads] - lam * o_all[n_heads:n_rows]
        oa_ref[0] = (_rms(o, SUBLN_EPS) * wsub_ref[...] * (1.0 - lam_init)).astype(oa_ref.dtype)


def _mlp_with_paged_attention(u, w_up, w_down, q_s, k_new, v_new, cache_k, cache_v, page_table, lams, w_subln, *,
                              layer, n_heads, dk, dv, lam_init, tf=512):
    m_rows, d = u.shape
    d_ff = w_up.shape[1]
    db, n_pages = page_table.shape
    depth, n_pool, page, _, _ = cache_k.shape
    assert m_rows % db == 0 and d_ff % tf == 0
    tm, n_groups = m_rows // db, d_ff // tf
    assert n_pages % n_groups == 0 and n_heads & (n_heads - 1) == 0
    pages_per_step = n_pages // n_groups
    rows = page * n_heads
    ck = cache_k.reshape(depth * n_pool * rows, 2 * dk)
    cv = cache_v.reshape(depth * n_pool * rows, dv)
    base = layer * n_pool

    def page_spec(r, width):
        return pl.BlockSpec((rows, width), lambda b, g, pt: (base + pt[b * n_pages + g * pages_per_step + r], 0))

    lam_spec = pl.BlockSpec((1, dk), lambda b, g, pt: (0, 0))
    tok_spec = pl.BlockSpec((1, n_heads, dv), lambda b, g, pt: (b, 0, 0))
    n_rows = 2 * n_heads
    grid_spec = pltpu.PrefetchScalarGridSpec(
        num_scalar_prefetch=1,
        grid=(db, n_groups),
        in_specs=[pl.BlockSpec((tm, d), lambda b, g, pt: (b, 0)),
                  pl.BlockSpec((d, tf), lambda b, g, pt: (0, g)),
                  pl.BlockSpec((tf, d), lambda b, g, pt: (g, 0)),
                  lam_spec, lam_spec, lam_spec, lam_spec,
                  pl.BlockSpec((1, dv), lambda b, g, pt: (0, 0)),
                  tok_spec, tok_spec, tok_spec]
                 + [page_spec(r, 2 * dk) for r in range(pages_per_step)]
                 + [page_spec(r, dv) for r in range(pages_per_step)],
        out_specs=[pl.BlockSpec((tm, d), lambda b, g, pt: (b, 0)), tok_spec],
        scratch_shapes=[
            pltpu.VMEM((n_rows, 2 * dk), BF16),
            pltpu.VMEM((n_rows, rows), F32),
            pltpu.VMEM((n_rows, 1), F32),
            pltpu.VMEM((n_rows, 1), F32),
            pltpu.VMEM((n_rows, dv), F32),
        ],
    )
    heads = lambda a: a.reshape(db, n_heads, dv)
    mlp_out, o_attn = pl.pallas_call(
        functools.partial(_mlp_paged_body, pages_per_step=pages_per_step, n_heads=n_heads, dk=dk, page=page,
                          lam_init=lam_init),
        grid_spec=grid_spec,
        out_shape=[jax.ShapeDtypeStruct((m_rows, d), F32), jax.ShapeDtypeStruct((db, n_heads, dv), BF16)],
        compiler_params=_params("arbitrary", "arbitrary"),
        name="relu2_mlp_with_sample_paged_attention",
    )(page_table.reshape(-1), u, w_up, w_down, *lams, w_subln, heads(q_s), heads(k_new), heads(v_new),
      *([ck] * pages_per_step), *([cv] * pages_per_step))
    return mlp_out, o_attn.reshape(db, n_heads * dv)


def _to_column(row_vec, eye):
    n = eye.shape[0]
    return jnp.sum(jnp.where(eye, jnp.broadcast_to(row_vec, (n, n)), 0.0), axis=-1, keepdims=True)


def _to_row(col_vec, eye):
    n = eye.shape[0]
    return jnp.sum(jnp.where(eye, jnp.broadcast_to(col_vec, (n, n)), 0.0), axis=0, keepdims=True)


def _ssm_step_body(xbc_ref, z_ref, dt_ref, sc_ref, st_ref, cw_ref, cb_ref, dtb_ref, alog_ref, dskip_ref, wn_ref,
                   e_ref, o_ref, st_out_ref, *, conv_w, d_inner, d_state, n_groups, heads_per_group, headdim):
    gw = heads_per_group * headdim
    conv = cb_ref[...] + xbc_ref[0] * cw_ref[conv_w - 1:conv_w, :]
    sc = sc_ref[0]
    for j in range(conv_w - 1):
        conv = conv + sc[j:j + 1, :] * cw_ref[j:j + 1, :]
    act = _silu(conv)
    xs = act[:, :d_inner]
    gn = n_groups * d_state

    dtv = _softplus(dt_ref[0] + dtb_ref[...])
    da = dtv * (-jnp.exp(alog_ref[...]))
    expand = e_ref[...]
    dt_x = jnp.dot(dtv, expand, precision=HIGHEST, preferred_element_type=F32)
    decay_x = jnp.exp(jnp.dot(da, expand, precision=HIGHEST, preferred_element_type=F32))
    xd = xs * dt_x

    eye = (lax.broadcasted_iota(jnp.int32, (LANES, LANES), 0)
           == lax.broadcasted_iota(jnp.int32, (LANES, LANES), 1))
    y_tiles = []
    for t in range(d_inner // LANES):
        g = (t * LANES) // gw
        rows = slice(t * LANES, (t + 1) * LANES)
        b_row = act[:, d_inner + g * d_state:d_inner + (g + 1) * d_state]
        c_row = act[:, d_inner + gn + g * d_state:d_inner + gn + (g + 1) * d_state]
        st = st_ref[0, rows, :] * _to_column(decay_x[:, rows], eye) + _to_column(xd[:, rows], eye) * b_row
        st_out_ref[0, rows, :] = st
        y_tiles.append(_to_row(jnp.sum(st * c_row, axis=-1, keepdims=True), eye))
    y = jnp.concatenate(y_tiles, axis=-1) + dskip_ref[...] * xs
    gated = y * _silu(z_ref[0])
    wn = wn_ref[...]
    outs = []
    for g in range(n_groups):
        gs = slice(g * gw, (g + 1) * gw)
        outs.append(_rms(gated[:, gs], NORM_EPS) * wn[:, gs])
    o_ref[0] = jnp.concatenate(outs, axis=-1).astype(o_ref.dtype)


def _sample_ssm(xbc, z, dt, state_conv, state_ssm, conv_w, conv_b, dt_bias_p, a_log_p, dskip_x, w_norm, expand, *,
                d_state, n_groups, n_heads, headdim):
    db, conv_dim = xbc.shape
    d_inner = n_heads * headdim
    cw = conv_w.shape[0]
    per_b = lambda *tail: pl.BlockSpec((1,) + tail, lambda b: (b,) + (0,) * len(tail))
    const = lambda shape: pl.BlockSpec(shape, lambda b: (0, 0))
    o, st = pl.pallas_call(
        functools.partial(_ssm_step_body, conv_w=cw, d_inner=d_inner, d_state=d_state, n_groups=n_groups,
                          heads_per_group=n_heads // n_groups, headdim=headdim),
        grid=(db,),
        in_specs=[
            per_b(1, conv_dim), per_b(1, d_inner), per_b(1, LANES), per_b(cw - 1, conv_dim),
            per_b(d_inner, d_state),
            const((cw, conv_dim)), const((1, conv_dim)), const((1, LANES)), const((1, LANES)),
            const((1, d_inner)), const((1, d_inner)), const((LANES, d_inner)),
        ],
        out_specs=[per_b(1, d_inner), per_b(d_inner, d_state)],
        out_shape=[jax.ShapeDtypeStruct((db, 1, d_inner), BF16),
                   jax.ShapeDtypeStruct((db, d_inner, d_state), F32)],
        compiler_params=_params("arbitrary"),
        name="sample_ssm_step",
    )(xbc.reshape(db, 1, conv_dim), z.reshape(db, 1, d_inner), dt.reshape(db, 1, LANES), state_conv,
      state_ssm.reshape(db, d_inner, d_state), conv_w, conv_b, dt_bias_p, a_log_p, dskip_x, w_norm, expand)
    return o.reshape(db, d_inner), st


def _pad_lanes(v):
    return jnp.pad(v.astype(F32), (0, LANES - v.shape[0])).reshape(1, LANES)


def kernel(x_prompt, x_sample, p_prompt, p_sample, cache_k, cache_v, page_table, state_ssm, state_conv, g_mix, w_in, lambda_q1, lambda_k1, lambda_q2, lambda_k2, w_subln, conv_w, conv_b, dt_bias, a_log, d_skip, w_ssm_norm, w_out, g_ffn, w_up, w_down, w_ple, w_ple_norm, w_ple_gate, g_final):
    batch, seq, d_model = x_prompt.shape
    db, dec_seq, _ = x_sample.shape
    assert dec_seq == 1, "the sample kernels handle one new token per sequence"
    depth = w_in.shape[0]
    n_heads_a = cache_k.shape[3]
    dv = cache_v.shape[4]
    dk = lambda_q1.shape[1]
    assert cache_k.shape[4] == 2 * dk == dv == LANES
    aw = n_heads_a * dv
    n_heads_s, headdim, d_state = state_ssm.shape[2:]
    d_inner = n_heads_s * headdim
    conv_dim = state_conv.shape[3]
    n_groups = (conv_dim - d_inner) // (2 * d_state)
    assert n_heads_s <= LANES and LANES % headdim == 0 and d_state == LANES
    n_main = 3 * aw + d_inner + conv_dim
    q_scale = float(dk) ** -0.5 * LOG2E
    expand = _head_expand(n_heads_s, headdim)

    hp = x_prompt.reshape(batch * seq, d_model)
    hs = x_sample.reshape(db, d_model)
    outs = {name: [] for name in ("kp", "vp", "sp", "cp", "ks", "vs", "ss", "cs")}
    for i in range(depth):
        lam_init = 0.8 - 0.6 * math.exp(-0.3 * i)
        last = i == depth - 1
        w_main = w_in[i].astype(BF16)
        w_dt = jnp.pad(w_in[i][:, n_main:].astype(BF16), ((0, 0), (0, LANES - n_heads_s)))
        w_out_i = w_out[i].astype(BF16)
        w_up_i, w_down_i = w_up[i].astype(BF16), w_down[i].astype(BF16)
        w_ple_i, w_gate_i = w_ple[i].astype(BF16), w_ple_gate[i].astype(BF16)
        g_mix_i, g_ffn_i = g_mix[i].reshape(1, -1), g_ffn[i].reshape(1, -1)
        lams = tuple(v[i].reshape(1, dk) for v in (lambda_q1, lambda_k1, lambda_q2, lambda_k2))
        w_subln_i = w_subln[i].reshape(1, dv)
        conv_b_i = conv_b[i].reshape(1, conv_dim)
        dt_bias_p, a_log_p = _pad_lanes(dt_bias[i]), _pad_lanes(a_log[i])
        dskip_x = jnp.repeat(d_skip[i].astype(F32), headdim).reshape(1, d_inner)
        w_norm_i = w_ssm_norm[i].reshape(1, d_inner)
        w_ple_norm_i = w_ple_norm[i].reshape(1, d_model)
        g_final_r = g_final.reshape(1, d_model)
        proj = functools.partial(_inproj, g=g_mix_i, w_main=w_main, w_dt=w_dt, widths=(aw, d_inner, conv_dim),
                                 q_scale=q_scale)
        ssm_kw = dict(d_state=d_state, n_groups=n_groups, n_heads=n_heads_s, headdim=headdim)

        ple = functools.partial(_ple, w_ple=w_ple_i, w_ple_norm=w_ple_norm_i, w_gate=w_gate_i, g_final=g_final_r,
                                final_norm=last)

        q_bf, k32, v32, k_bf, v_bf, z, xbc, dt = proj(hp)
        o_attn = _prompt_attention(q_bf, k_bf, v_bf, lams, w_subln_i, batch=batch, seq=seq, n_heads=n_heads_a,
                                   dk=dk, dv=dv, lam_init=lam_init)
        o_ssm, st = _prompt_ssd(xbc, z, dt, conv_w[i], conv_b_i, dt_bias_p, a_log_p, dskip_x, w_norm_i, expand,
                                batch=batch, seq=seq, **ssm_kw)
        h1, u = _outproj(o_attn, o_ssm, hp, w_out_i, g_ffn_i)
        outs["kp"].append(k32.reshape(batch, seq, n_heads_a, 2 * dk))
        outs["vp"].append(v32.reshape(batch, seq, n_heads_a, dv))
        outs["sp"].append(st.reshape(batch, n_heads_s, headdim, d_state))
        cw = conv_w.shape[1]
        outs["cp"].append(xbc.reshape(batch, seq, conv_dim)[:, seq - (cw - 1):])
        qs_bf, ks32, vs32, _, _, zs, xbcs, dts = proj(hs)

        mlp_out, os_attn = _mlp_with_paged_attention(u, w_up_i, w_down_i, qs_bf, ks32, vs32, cache_k, cache_v,
                                                     page_table, lams, w_subln_i, layer=i, n_heads=n_heads_a, dk=dk,
                                                     dv=dv, lam_init=lam_init)
        hp = ple(h1, mlp_out, p_prompt[i].reshape(batch * seq, -1))

        os_ssm, st = _sample_ssm(xbcs, zs, dts, state_conv[i], state_ssm[i], conv_w[i], conv_b_i, dt_bias_p, a_log_p,
                                 dskip_x, w_norm_i, expand, **ssm_kw)
        h1, u = _outproj(os_attn, os_ssm, hs, w_out_i, g_ffn_i)
        hs = ple(h1, _mlp(u, w_up_i, w_down_i), p_sample[i].reshape(db, -1))
        outs["ks"].append(ks32.reshape(db, 1, n_heads_a, 2 * dk))
        outs["vs"].append(vs32.reshape(db, 1, n_heads_a, dv))
        outs["ss"].append(st.reshape(db, n_heads_s, headdim, d_state))
        outs["cs"].append(jnp.concatenate([state_conv[i][:, 1:], xbcs[:, None, :]], axis=1))

    stack = lambda name: jnp.stack(outs[name])
    return (hp.reshape(batch, seq, d_model), hs.reshape(db, 1, d_model), stack("kp"), stack("vp"), stack("sp"),
            stack("cp"), stack("ks"), stack("vs"), stack("ss"), stack("cs"))
```

```python
import functools
import math

import jax
import jax.numpy as jnp
import numpy as np
from jax import lax
from jax.experimental import pallas as pl
from jax.experimental.pallas import tpu as pltpu

F32 = jnp.float32
BF16 = jnp.bfloat16
NORM_EPS = 1e-6
SUBLN_EPS = 1e-5
HIGHEST = lax.Precision.HIGHEST
LOG2E = math.log2(math.e)

LANES = 128
SUBLANES = 8
VMEM_LIMIT_BYTES = 56 * 1024 * 1024


def _params(*semantics):
    return pltpu.CompilerParams(dimension_semantics=semantics, vmem_limit_bytes=VMEM_LIMIT_BYTES)


def _rms(x, eps):
    return x * lax.rsqrt(jnp.mean(x * x, axis=-1, keepdims=True) + eps)


def _silu(x):
    h = 0.5 * x
    return h + h * jnp.tanh(h)


def _softplus(x):
    return jnp.maximum(x, 0.0) + jnp.log1p(jnp.exp(-jnp.abs(x)))


def _row_tile(rows, target):
    t = min(rows, target)
    assert rows % t == 0, (rows, t)
    return t


def _inproj_body(x_ref, g_ref, w_ref, wdt_ref, q_ref, k32_ref, v32_ref, kbf_ref, vbf_ref, z_ref,
                 xbc_ref, dt_ref, u_ref, *, bounds, q_scale):
    j = pl.program_id(1)

    @pl.when(j == 0)
    def _():
        u = _rms(x_ref[...], NORM_EPS) * g_ref[...]
        u_ref[...] = u.astype(BF16)
        dt_ref[...] = jnp.dot(u_ref[...], wdt_ref[...], preferred_element_type=F32)

    def tile():
        return jnp.dot(u_ref[...], w_ref[...], preferred_element_type=F32)

    b_q, b_k, b_v, b_z = bounds

    @pl.when(j < b_q)
    def _():
        q_ref[...] = (tile() * q_scale).astype(BF16)

    @pl.when((j >= b_q) & (j < b_k))
    def _():
        res = tile()
        k32_ref[...] = res
        kbf_ref[...] = res.astype(BF16)

    @pl.when((j >= b_k) & (j < b_v))
    def _():
        res = tile()
        v32_ref[...] = res
        vbf_ref[...] = res.astype(BF16)

    @pl.when((j >= b_v) & (j < b_z))
    def _():
        z_ref[...] = tile()

    @pl.when(j >= b_z)
    def _():
        xbc_ref[...] = tile()


def _inproj(x, g, w_main, w_dt, *, widths, q_scale, tm_target=1024, tn=512):
    rows, d = x.shape
    tm = _row_tile(rows, tm_target)
    aw, d_inner, conv_dim = widths
    seg = [aw, aw, aw, d_inner, conv_dim]
    assert all(s % tn == 0 for s in seg)
    tiles = [s // tn for s in seg]
    starts = [sum(tiles[:i]) for i in range(len(tiles))]
    n_tiles = sum(tiles)
    bounds = tuple(starts[i] + tiles[i] for i in range(4))

    def seg_map(s):
        lo, n = starts[s], tiles[s]
        return lambda i, j: (i, jnp.clip(j - lo, 0, n - 1))

    def out(s, dtype):
        return jax.ShapeDtypeStruct((rows, seg[s]), dtype), pl.BlockSpec((tm, tn), seg_map(s))

    outs = [out(0, BF16), out(1, F32), out(2, F32), out(1, BF16), out(2, BF16), out(3, F32), out(4, F32)]
    out_shape = [o[0] for o in outs] + [jax.ShapeDtypeStruct((rows, LANES), F32)]
    out_specs = [o[1] for o in outs] + [pl.BlockSpec((tm, LANES), lambda i, j: (i, 0))]
    return pl.pallas_call(
        functools.partial(_inproj_body, bounds=bounds, q_scale=q_scale),
        grid=(rows // tm, n_tiles),
        in_specs=[
            pl.BlockSpec((tm, d), lambda i, j: (i, 0)),
            pl.BlockSpec((1, d), lambda i, j: (0, 0)),
            pl.BlockSpec((d, tn), lambda i, j: (0, j)),
            pl.BlockSpec((d, LANES), lambda i, j: (0, 0)),
        ],
        out_specs=out_specs,
        out_shape=out_shape,
        scratch_shapes=[pltpu.VMEM((tm, d), BF16)],
        compiler_params=_params("arbitrary", "arbitrary"),
        name="norm_inproj",
    )(x, g, w_main, w_dt)


def _lambda_value(lq1_ref, lk1_ref, lq2_ref, lk2_ref, lam_init):
    s1 = jnp.sum(lq1_ref[...] * lk1_ref[...], axis=-1, keepdims=True)
    s2 = jnp.sum(lq2_ref[...] * lk2_ref[...], axis=-1, keepdims=True)
    return jnp.exp(s1) - jnp.exp(s2) + lam_init


def _attn_body(coef_ref, lq1_ref, lk1_ref, lq2_ref, lk2_ref, wsub_ref, qf_ref, kf_ref, q_ref, k_ref, v_ref, o_ref,
               vt_ref, q2_ref, s_ref, top_ref, m_ref, acc_ref, *, tq, dk, dv, heads, lam_init):
    hg = pl.program_id(1)
    qi = pl.program_id(2)
    cols2 = 2 * tq
    n_kb, tk = vt_ref.shape[1], vt_ref.shape[3]
    assert tk == tq
    dvx = vt_ref.shape[2]

    @pl.when(qi == 0)
    def _():
        for hb in range(heads):
            for kb in range(n_kb):
                for part in range(tk // LANES):
                    rows = slice(kb * tk + part * LANES, kb * tk + (part + 1) * LANES)
                    vt = v_ref[rows, hb * dv:(hb + 1) * dv].astype(F32).T
                    vt_ref[hb, kb, 0:dv, part * LANES:(part + 1) * LANES] = vt.astype(BF16)
                vt_ref[hb, kb, dv:dvx, :] = jnp.ones((dvx - dv, tk), BF16)
            q2_ref[hb, 0:2 * dk, :] = jnp.zeros((2 * dk, cols2), BF16)
            q2_ref[hb, 2 * dk:, 0:tq] = qf_ref[hb]
            q2_ref[hb, 2 * dk:, tq:cols2] = qf_ref[hb]

    for hb in range(heads):
        qt = q_ref[:, hb * 2 * dk:(hb + 1) * 2 * dk].astype(F32).T.astype(BF16)
        q2_ref[hb, 0:dk, 0:tq] = qt[0:dk]
        q2_ref[hb, dk:2 * dk, tq:cols2] = qt[dk:2 * dk]
    m_ref[...] = jnp.full(m_ref.shape, -jnp.inf, F32)
    acc_ref[...] = jnp.zeros(acc_ref.shape, F32)

    def score(hb, kb, slot):
        start = pl.multiple_of(kb * tk, tk)
        kx = jnp.concatenate([k_ref[pl.ds(start, tk), hb * 2 * dk:(hb + 1) * 2 * dk], kf_ref[hb]], axis=-1)
        s = jnp.dot(kx, q2_ref[hb], preferred_element_type=F32)
        s_ref[hb, slot] = s
        top_ref[hb, slot] = jnp.max(s, axis=0, keepdims=True)

    def accumulate(hb, kb, slot, masked):
        s = s_ref[hb, slot]
        off = -coef_ref[hg * heads + hb] * ((qi - kb) * tq).astype(F32)
        if masked:
            key = lax.broadcasted_iota(jnp.int32, (tk, cols2), 0)
            col = lax.broadcasted_iota(jnp.int32, (tk, cols2), 1)
            s = jnp.where(jnp.where(col >= tq, col - tq, col) >= key, s, -jnp.inf)
            top = jnp.max(s, axis=0, keepdims=True)
        else:
            top = top_ref[hb, slot]
        m_old = m_ref[hb]
        m_new = jnp.maximum(m_old, top + off)
        p = jnp.exp2(s - (m_new - off)).astype(BF16)
        alpha = jnp.exp2(m_old - m_new)
        acc_ref[hb] = alpha * acc_ref[hb] + jnp.dot(vt_ref[hb, kb], p, preferred_element_type=F32)
        m_ref[hb] = m_new

    def stage(kb_next, kb, slot, masked=False):
        for hb in range(heads):
            if kb_next is not None:
                score(hb, kb_next, 1 - slot)
            accumulate(hb, kb, slot, masked)

    def pair(j, carry):
        stage(2 * j + 1, 2 * j, 0)
        stage(2 * j + 2, 2 * j + 1, 1)
        return carry

    for hb in range(heads):
        score(hb, 0, 0)
    lax.fori_loop(0, qi // 2, pair, 0)

    @pl.when(qi % 2 == 0)
    def _():
        stage(None, qi, 0, masked=True)

    @pl.when(qi % 2 == 1)
    def _():
        stage(qi, qi - 1, 0)
        stage(None, qi, 1, masked=True)

    lam = _lambda_value(lq1_ref, lk1_ref, lq2_ref, lk2_ref, lam_init)
    for hb in range(heads):
        acc = acc_ref[hb]
        o = acc[0:dv] / acc[dv:dv + 1]
        o = o[:, 0:tq] - lam * o[:, tq:cols2]
        o = o * lax.rsqrt(jnp.mean(o * o, axis=0, keepdims=True) + SUBLN_EPS)
        o = o * wsub_ref[...] * (1.0 - lam_init)
        o_ref[:, hb * dv:(hb + 1) * dv] = o.T.astype(o_ref.dtype)


def _bf16_split3(v):
    hi = v.astype(BF16)
    r1 = v - hi.astype(np.float32)
    mid = r1.astype(BF16)
    lo = (r1 - mid.astype(np.float32)).astype(BF16)
    return [hi, mid, lo]


def _alibi_lanes(n_heads, t):
    slopes = np.float32(2.0) ** (np.float32(-8.0) * np.arange(1, n_heads + 1, dtype=np.float32) / np.float32(n_heads))
    coef = (slopes * np.float32(LOG2E)).astype(np.float32)
    cpos = coef[:, None] * np.arange(t, dtype=np.float32)[None, :]
    ones = [np.ones((n_heads, t), BF16)] * 3
    pad = [np.zeros((n_heads, t), BF16)] * (LANES - 6)
    q_lanes = np.stack(_bf16_split3(-cpos) + ones + pad, axis=1)
    k_lanes = np.stack(ones + _bf16_split3(cpos) + pad, axis=-1)
    return jnp.asarray(coef), jnp.asarray(q_lanes), jnp.asarray(k_lanes)


def _prompt_attention(q_bf, k_bf, v_bf, lams, w_subln, *, batch, seq, n_heads, dk, dv, lam_init, tq=256, heads=4):
    tq = _row_tile(seq, tq)
    nq = seq // tq
    assert n_heads % heads == 0
    coef, q_lanes, k_lanes = _alibi_lanes(n_heads, tq)
    ones_rows = 2 * SUBLANES
    lam_spec = pl.BlockSpec((1, dk), lambda b, h, i: (0, 0))
    qlanes_spec = pl.BlockSpec((heads, LANES, tq), lambda b, h, i: (h, 0, 0))
    klanes_spec = pl.BlockSpec((heads, tq, LANES), lambda b, h, i: (h, 0, 0))
    return pl.pallas_call(
        functools.partial(_attn_body, tq=tq, dk=dk, dv=dv, heads=heads, lam_init=lam_init),
        grid=(batch, n_heads // heads, nq),
        in_specs=[
            pl.BlockSpec(memory_space=pltpu.SMEM),
            lam_spec, lam_spec, lam_spec, lam_spec,
            pl.BlockSpec((dv, 1), lambda b, h, i: (0, 0)),
            qlanes_spec, klanes_spec,
            pl.BlockSpec((tq, heads * 2 * dk), lambda b, h, i: (b * nq + i, h)),
            pl.BlockSpec((seq, heads * 2 * dk), lambda b, h, i: (b, h)),
            pl.BlockSpec((seq, heads * dv), lambda b, h, i: (b, h)),
        ],
        out_specs=pl.BlockSpec((tq, heads * dv), lambda b, h, i: (b * nq + i, h)),
        out_shape=jax.ShapeDtypeStruct((batch * seq, n_heads * dv), BF16),
        scratch_shapes=[
            pltpu.VMEM((heads, nq, dv + ones_rows, tq), BF16),
            pltpu.VMEM((heads, 2 * dk + LANES, 2 * tq), BF16),
            pltpu.VMEM((heads, 2, tq, 2 * tq), F32),
            pltpu.VMEM((heads, 2, 1, 2 * tq), F32),
            pltpu.VMEM((heads, 1, 2 * tq), F32),
            pltpu.VMEM((heads, dv + ones_rows, 2 * tq), F32),
        ],
        compiler_params=_params("arbitrary", "arbitrary", "arbitrary"),
        name="prompt_diff_attention",
    )(coef, *lams, w_subln.reshape(dv, 1), q_lanes, k_lanes, q_bf, k_bf, v_bf)


def _ssd_body(xbc_ref, z_ref, dt_ref, cw_ref, cb_ref, dtb_ref, alog_ref, dskip_ref, wn_ref, e_ref,
              o_ref, st_out_ref, ext_ref, st_ref, y_ref, *, chunk, conv_w, d_inner, d_state, n_groups,
              heads_per_group, headdim):
    c = pl.program_id(1)
    n_chunks = pl.num_programs(1)
    pad = SUBLANES
    gw = heads_per_group * headdim

    @pl.when(c == 0)
    def _():
        ext_ref[0:pad, :] = jnp.zeros((pad, ext_ref.shape[1]), F32)
        st_ref[...] = jnp.zeros(st_ref.shape, F32)

    ext_ref[pad:pad + chunk, :] = xbc_ref[...]
    ext = ext_ref[...]
    conv = cb_ref[...] + ext[pad:pad + chunk] * cw_ref[conv_w - 1:conv_w, :]
    for j in range(conv_w - 1):
        shifted = pltpu.roll(ext, conv_w - 1 - j, axis=0)
        conv = conv + shifted[pad:pad + chunk] * cw_ref[j:j + 1, :]
    ext_ref[0:pad, :] = ext[chunk:chunk + pad]
    act = _silu(conv)

    xs = act[:, :d_inner]
    gn = n_groups * d_state
    bm = act[:, d_inner:d_inner + gn]
    cm = act[:, d_inner + gn:d_inner + 2 * gn]

    dtv = _softplus(dt_ref[...] + dtb_ref[...])
    da = dtv * (-jnp.exp(alog_ref[...]))
    r = lax.broadcasted_iota(jnp.int32, (chunk, chunk), 0)
    s = lax.broadcasted_iota(jnp.int32, (chunk, chunk), 1)
    causal = r >= s
    tri = causal.astype(F32).astype(BF16)
    a_cs = jnp.dot(jnp.concatenate([tri, tri, tri], axis=1), jnp.concatenate(_split3(da), axis=0),
                   preferred_element_type=F32)
    a_cs_t = a_cs.T

    per_head = jnp.concatenate([dtv, jnp.exp(a_cs), jnp.exp(a_cs[chunk - 1:chunk, :] - a_cs)], axis=0)
    spread = jnp.dot(jnp.concatenate(_split3(per_head), axis=1), e_ref[...], preferred_element_type=F32)
    dt_x = spread[0:chunk]
    eacs_x = spread[chunk:2 * chunk]
    xd = xs * dt_x
    xd_b = xd.astype(BF16)
    xdd_b = (xd * spread[2 * chunk:3 * chunk]).astype(BF16)
    chunk_decay_x = eacs_x[chunk - 1:chunk, :]

    lane = lax.broadcasted_iota(jnp.int32, (chunk, LANES), 1)
    heads_per_tile = LANES // headdim
    for g in range(n_groups):
        bm_g = bm[:, g * d_state:(g + 1) * d_state]
        cm_b = cm[:, g * d_state:(g + 1) * d_state].astype(BF16)
        cb = lax.dot_general(cm_b, bm_g.astype(BF16), (((1,), (1,)), ((), ())), preferred_element_type=F32)
        gs = slice(g * gw, (g + 1) * gw)
        st_g = st_ref[g]
        y_off = jnp.dot(cm_b, st_g.astype(BF16), preferred_element_type=F32) * eacs_x[:, gs]
        new = jnp.dot(bm_g.T.astype(BF16), xdd_b[:, gs], preferred_element_type=F32)
        st_ref[g] = st_g * chunk_decay_x[:, gs] + new
        for t in range(gw // LANES):
            lo = g * gw + t * LANES
            xd_t = xd_b[:, lo:lo + LANES]
            y_t = y_off[:, t * LANES:(t + 1) * LANES]
            for k in range(heads_per_tile):
                hh = g * heads_per_group + t * heads_per_tile + k
                seg = a_cs[:, hh:hh + 1] - a_cs_t[hh:hh + 1, :]
                lmat = jnp.exp(jnp.where(causal, seg, -jnp.inf))
                in_head = (lane >= k * headdim) & (lane < (k + 1) * headdim)
                xd_h = jnp.where(in_head, xd_t, jnp.zeros_like(xd_t))
                y_t = y_t + jnp.dot((cb * lmat).astype(BF16), xd_h, preferred_element_type=F32)
            y_ref[:, lo:lo + LANES] = y_t

    y = y_ref[...] + dskip_ref[...] * xs
    gated = y * _silu(z_ref[...])
    wn = wn_ref[...]
    for g in range(n_groups):
        gs = slice(g * gw, (g + 1) * gw)
        o_ref[:, gs] = (_rms(gated[:, gs], NORM_EPS) * wn[:, gs]).astype(o_ref.dtype)

    @pl.when(c == n_chunks - 1)
    def _():
        for g in range(n_groups):
            st_out_ref[0, g * gw:(g + 1) * gw, :] = st_ref[g].T


def _split3(x):
    hi = x.astype(BF16)
    r1 = x - hi.astype(F32)
    mid = r1.astype(BF16)
    lo = (r1 - mid.astype(F32)).astype(BF16)
    return [hi, mid, lo]


def _head_expand(n_heads, headdim):
    head_of_lane = jnp.arange(n_heads * headdim) // headdim
    return (jnp.arange(LANES)[:, None] == head_of_lane[None, :]).astype(F32)


def _prompt_ssd(xbc, z, dt, conv_w, conv_b, dt_bias_p, a_log_p, dskip_x, w_norm, expand, *, batch, seq,
                d_state, n_groups, n_heads, headdim, chunk=128):
    chunk = _row_tile(seq, chunk)
    nc = seq // chunk
    conv_dim = xbc.shape[1]
    d_inner = n_heads * headdim
    cw = conv_w.shape[0]
    hpg = n_heads // n_groups
    const = lambda shape: pl.BlockSpec(shape, lambda b, c: (0, 0))
    row_map = lambda b, c: (b * nc + c, 0)
    return pl.pallas_call(
        functools.partial(_ssd_body, chunk=chunk, conv_w=cw, d_inner=d_inner, d_state=d_state,
                          n_groups=n_groups, heads_per_group=hpg, headdim=headdim),
        grid=(batch, nc),
        in_specs=[
            pl.BlockSpec((chunk, conv_dim), row_map),
            pl.BlockSpec((chunk, d_inner), row_map),
            pl.BlockSpec((chunk, LANES), row_map),
            const((cw, conv_dim)), const((1, conv_dim)), const((1, LANES)), const((1, LANES)),
            const((1, d_inner)), const((1, d_inner)), const((3 * LANES, d_inner)),
        ],
        out_specs=[
            pl.BlockSpec((chunk, d_inner), row_map),
            pl.BlockSpec((1, d_inner, d_state), lambda b, c: (b, 0, 0)),
        ],
        out_shape=[
            jax.ShapeDtypeStruct((batch * seq, d_inner), BF16),
            jax.ShapeDtypeStruct((batch, d_inner, d_state), F32),
        ],
        scratch_shapes=[
            pltpu.VMEM((chunk + SUBLANES, conv_dim), F32),
            pltpu.VMEM((n_groups, d_state, hpg * headdim), F32),
            pltpu.VMEM((chunk, d_inner), F32),
        ],
        compiler_params=_params("arbitrary", "arbitrary"),
        name="prompt_ssd",
    )(xbc, z, dt, conv_w, conv_b, dt_bias_p, a_log_p, dskip_x, w_norm, jnp.tile(expand, (3, 1)).astype(BF16))


def _outproj_body(oa_ref, os_ref, x_ref, wa_ref, ws_ref, g_ref, h_ref, u_ref):
    h = x_ref[...] + jnp.dot(oa_ref[...], wa_ref[...], preferred_element_type=F32)
    h = h + jnp.dot(os_ref[...], ws_ref[...], preferred_element_type=F32)
    h_ref[...] = h
    u_ref[...] = (_rms(h, NORM_EPS) * g_ref[...]).astype(u_ref.dtype)


def _outproj(o_attn, o_ssm, x, w_out, g, *, tm_target=512):
    rows, d = x.shape
    tm = _row_tile(rows, tm_target)
    wa, ws = o_attn.shape[1], o_ssm.shape[1]
    assert wa == ws and w_out.shape[0] == wa + ws
    row = lambda w: pl.BlockSpec((tm, w), lambda i: (i, 0))
    return pl.pallas_call(
        _outproj_body,
        grid=(rows // tm,),
        in_specs=[row(wa), row(ws), row(d),
                  pl.BlockSpec((wa, d), lambda i: (0, 0)), pl.BlockSpec((ws, d), lambda i: (1, 0)),
                  pl.BlockSpec((1, d), lambda i: (0, 0))],
        out_specs=[row(d), row(d)],
        out_shape=[jax.ShapeDtypeStruct((rows, d), F32), jax.ShapeDtypeStruct((rows, d), BF16)],
        compiler_params=_params("arbitrary"),
        name="outproj_residual_norm",
    )(o_attn, o_ssm, x, w_out, w_out, g)


def _mlp_accumulate(u_ref, wu_ref, wd_ref, o_ref):
    a = jnp.dot(u_ref[...], wu_ref[...], preferred_element_type=F32)
    a = jnp.square(jnp.maximum(a, 0.0)).astype(BF16)
    o_ref[...] += jnp.dot(a, wd_ref[...], preferred_element_type=F32)


def _mlp_step(u_ref, wu_ref, wd_ref, o_ref):
    @pl.when(pl.program_id(1) == 0)
    def _():
        o_ref[...] = jnp.zeros(o_ref.shape, F32)

    _mlp_accumulate(u_ref, wu_ref, wd_ref, o_ref)


def _mlp(u, w_up, w_down, *, tm_target=1024, tf=512):
    rows, d = u.shape
    d_ff = w_up.shape[1]
    tm = _row_tile(rows, tm_target)
    assert d_ff % tf == 0
    return pl.pallas_call(
        _mlp_step,
        grid=(rows // tm, d_ff // tf),
        in_specs=[
            pl.BlockSpec((tm, d), lambda i, f: (i, 0)),
            pl.BlockSpec((d, tf), lambda i, f: (0, f)),
            pl.BlockSpec((tf, d), lambda i, f: (f, 0)),
        ],
        out_specs=pl.BlockSpec((tm, d), lambda i, f: (i, 0)),
        out_shape=jax.ShapeDtypeStruct((rows, d), F32),
        compiler_params=_params("arbitrary", "arbitrary"),
        name="relu2_mlp",
    )(u, w_up, w_down)


def _ple_body(h_ref, m_ref, p_ref, wp_ref, wpn_ref, wg_ref, gf_ref, o_ref, *, final_norm):
    h = h_ref[...] + m_ref[...]
    e = jnp.dot(p_ref[...].astype(BF16), wp_ref[...], preferred_element_type=F32)
    e = _rms(e, NORM_EPS) * wpn_ref[...]
    gate = jax.nn.sigmoid(jnp.dot(h.astype(BF16), wg_ref[...], preferred_element_type=F32))
    h = h + e * gate
    if final_norm:
        h = _rms(h, NORM_EPS) * gf_ref[...]
    o_ref[...] = h


def _ple(h, mlp_out, p, w_ple, w_ple_norm, w_gate, g_final, *, final_norm, tm_target=512):
    rows, d = h.shape
    pd = p.shape[1]
    tm = _row_tile(rows, tm_target)
    row = lambda w: pl.BlockSpec((tm, w), lambda i: (i, 0))
    const = lambda shape: pl.BlockSpec(shape, lambda i: (0, 0))
    return pl.pallas_call(
        functools.partial(_ple_body, final_norm=final_norm),
        grid=(rows // tm,),
        in_specs=[row(d), row(d), row(pd), const((pd, d)), const((1, d)), const((d, d)), const((1, d))],
        out_specs=row(d),
        out_shape=jax.ShapeDtypeStruct((rows, d), F32),
        compiler_params=_params("arbitrary"),
        name="ple_gate",
    )(h, mlp_out, p, w_ple, w_ple_norm, w_gate, g_final)


def _mlp_paged_body(pt_ref, u_ref, wu_ref, wd_ref, lq1_ref, lk1_ref, lq2_ref, lk2_ref, wsub_ref, q_ref, kn_ref, vn_ref,
                    *rest, pages_per_step, n_heads, dk, page, lam_init):
    del pt_ref
    k_refs = rest[:pages_per_step]
    v_refs = rest[pages_per_step:2 * pages_per_step]
    o_ref, oa_ref, qm_ref, base_ref, m_ref, l_ref, acc_ref = rest[2 * pages_per_step:]
    g = pl.program_id(1)
    n_rows = 2 * n_heads
    n_cols = page * n_heads
    head_bits = int(math.log2(n_heads))
    past_len = pl.num_programs(1) * pages_per_step * page

    rowi = lax.broadcasted_iota(jnp.int32, (n_rows, 1), 0)
    head = rowi & (n_heads - 1)
    coef = jnp.exp2(-8.0 * (head + 1).astype(F32) / n_heads) * LOG2E

    @pl.when(g == 0)
    def _():
        o_ref[...] = jnp.zeros(o_ref.shape, F32)
        q = q_ref[0].astype(F32)
        lane = lax.broadcasted_iota(jnp.int32, q.shape, 1)
        qm = jnp.concatenate([jnp.where(lane < dk, q, 0.0), jnp.where(lane >= dk, q, 0.0)], axis=0)
        qm_ref[...] = qm.astype(BF16)
        col = lax.broadcasted_iota(jnp.int32, (n_rows, n_cols), 1)
        own = (col & (n_heads - 1)) == head
        tok = lax.shift_right_logical(col, head_bits)
        dist = (past_len - tok).astype(F32)
        base_ref[...] = jnp.where(own, -coef * dist, -jnp.inf)
        m_ref[...] = jnp.full(m_ref.shape, -jnp.inf, F32)
        l_ref[...] = jnp.zeros(l_ref.shape, F32)
        acc_ref[...] = jnp.zeros(acc_ref.shape, F32)

    qm = qm_ref[...]
    scores, offs, top = [], [], None
    for r in range(pages_per_step):
        off = coef * ((g * pages_per_step + r) * page).astype(F32)
        kp = k_refs[r][...].astype(BF16)
        s = lax.dot_general(qm, kp, (((1,), (1,)), ((), ())), preferred_element_type=F32) + base_ref[...]
        cand = jnp.max(s, axis=-1, keepdims=True) + off
        top = cand if top is None else jnp.maximum(top, cand)
        scores.append(s)
        offs.append(off)
    m_old = m_ref[...]
    m_new = jnp.maximum(m_old, top)
    alpha = jnp.exp2(m_old - m_new)
    l_new = alpha * l_ref[...]
    acc_new = alpha * acc_ref[...]
    for r in range(pages_per_step):
        p = jnp.exp2(scores[r] - (m_new - offs[r]))
        l_new = l_new + jnp.sum(p, axis=-1, keepdims=True)
        acc_new = acc_new + jnp.dot(p.astype(BF16), v_refs[r][...].astype(BF16), preferred_element_type=F32)
    l_ref[...] = l_new
    acc_ref[...] = acc_new
    m_ref[...] = m_new

    _mlp_accumulate(u_ref, wu_ref, wd_ref, o_ref)

    @pl.when(g == pl.num_programs(1) - 1)
    def _():
        kn = jnp.concatenate([kn_ref[0], kn_ref[0]], axis=0)
        vn = jnp.concatenate([vn_ref[0], vn_ref[0]], axis=0)
        s_new = jnp.sum(qm_ref[...].astype(F32) * kn, axis=-1, keepdims=True)
        m_old = m_ref[...]
        m_new = jnp.maximum(m_old, s_new)
        p_new = jnp.exp2(s_new - m_new)
        alpha = jnp.exp2(m_old - m_new)
        o_all = (alpha * acc_ref[...] + p_new * vn) / (alpha * l_ref[...] + p_new)
        lam = _lambda_value(lq1_ref, lk1_ref, lq2_ref, lk2_ref, lam_init)
        o = o_all[0:n_heads] - lam * o_all[n_heads:n_rows]
        oa_ref[0] = (_rms(o, SUBLN_EPS) * wsub_ref[...] * (1.0 - lam_init)).astype(oa_ref.dtype)


def _mlp_with_paged_attention(u, w_up, w_down, q_s, k_new, v_new, cache_k, cache_v, page_table, lams, w_subln, *,
                              layer, n_heads, dk, dv, lam_init, tf=512):
    m_rows, d = u.shape
    d_ff = w_up.shape[1]
    db, n_pages = page_table.shape
    depth, n_pool, page, _, _ = cache_k.shape
    assert m_rows % db == 0 and d_ff % tf == 0
    tm, n_groups = m_rows // db, d_ff // tf
    assert n_pages % n_groups == 0 and n_heads & (n_heads - 1) == 0
    pages_per_step = n_pages // n_groups
    rows = page * n_heads
    ck = cache_k.reshape(depth * n_pool * rows, 2 * dk)
    cv = cache_v.reshape(depth * n_pool * rows, dv)
    base = layer * n_pool

    def page_spec(r, width):
        return pl.BlockSpec((rows, width), lambda b, g, pt: (base + pt[b * n_pages + g * pages_per_step + r], 0))

    lam_spec = pl.BlockSpec((1, dk), lambda b, g, pt: (0, 0))
    tok_spec = pl.BlockSpec((1, n_heads, dv), lambda b, g, pt: (b, 0, 0))
    n_rows = 2 * n_heads
    grid_spec = pltpu.PrefetchScalarGridSpec(
        num_scalar_prefetch=1,
        grid=(db, n_groups),
        in_specs=[pl.BlockSpec((tm, d), lambda b, g, pt: (b, 0)),
                  pl.BlockSpec((d, tf), lambda b, g, pt: (0, g)),
                  pl.BlockSpec((tf, d), lambda b, g, pt: (g, 0)),
                  lam_spec, lam_spec, lam_spec, lam_spec,
                  pl.BlockSpec((1, dv), lambda b, g, pt: (0, 0)),
                  tok_spec, tok_spec, tok_spec]
                 + [page_spec(r, 2 * dk) for r in range(pages_per_step)]
                 + [page_spec(r, dv) for r in range(pages_per_step)],
        out_specs=[pl.BlockSpec((tm, d), lambda b, g, pt: (b, 0)), tok_spec],
        scratch_shapes=[
            pltpu.VMEM((n_rows, 2 * dk), BF16),
            pltpu.VMEM((n_rows, rows), F32),
            pltpu.VMEM((n_rows, 1), F32),
            pltpu.VMEM((n_rows, 1), F32),
            pltpu.VMEM((n_rows, dv), F32),
        ],
    )
    heads = lambda a: a.reshape(db, n_heads, dv)
    mlp_out, o_attn = pl.pallas_call(
        functools.partial(_mlp_paged_body, pages_per_step=pages_per_step, n_heads=n_heads, dk=dk, page=page,
                          lam_init=lam_init),
        grid_spec=grid_spec,
        out_shape=[jax.ShapeDtypeStruct((m_rows, d), F32), jax.ShapeDtypeStruct((db, n_heads, dv), BF16)],
        compiler_params=_params("arbitrary", "arbitrary"),
        name="relu2_mlp_with_sample_paged_attention",
    )(page_table.reshape(-1), u, w_up, w_down, *lams, w_subln, heads(q_s), heads(k_new), heads(v_new),
      *([ck] * pages_per_step), *([cv] * pages_per_step))
    return mlp_out, o_attn.reshape(db, n_heads * dv)


def _to_column(row_vec, eye):
    n = eye.shape[0]
    return jnp.sum(jnp.where(eye, jnp.broadcast_to(row_vec, (n, n)), 0.0), axis=-1, keepdims=True)


def _to_row(col_vec, eye):
    n = eye.shape[0]
    return jnp.sum(jnp.where(eye, jnp.broadcast_to(col_vec, (n, n)), 0.0), axis=0, keepdims=True)


def _ssm_step_body(xbc_ref, z_ref, dt_ref, sc_ref, st_ref, cw_ref, cb_ref, dtb_ref, alog_ref, dskip_ref, wn_ref,
                   e_ref, o_ref, st_out_ref, *, conv_w, d_inner, d_state, n_groups, heads_per_group, headdim):
    gw = heads_per_group * headdim
    conv = cb_ref[...] + xbc_ref[0] * cw_ref[conv_w - 1:conv_w, :]
    sc = sc_ref[0]
    for j in range(conv_w - 1):
        conv = conv + sc[j:j + 1, :] * cw_ref[j:j + 1, :]
    act = _silu(conv)
    xs = act[:, :d_inner]
    gn = n_groups * d_state

    dtv = _softplus(dt_ref[0] + dtb_ref[...])
    da = dtv * (-jnp.exp(alog_ref[...]))
    expand = e_ref[...]
    dt_x = jnp.dot(dtv, expand, precision=HIGHEST, preferred_element_type=F32)
    decay_x = jnp.exp(jnp.dot(da, expand, precision=HIGHEST, preferred_element_type=F32))
    xd = xs * dt_x

    eye = (lax.broadcasted_iota(jnp.int32, (LANES, LANES), 0)
           == lax.broadcasted_iota(jnp.int32, (LANES, LANES), 1))
    y_tiles = []
    for t in range(d_inner // LANES):
        g = (t * LANES) // gw
        rows = slice(t * LANES, (t + 1) * LANES)
        b_row = act[:, d_inner + g * d_state:d_inner + (g + 1) * d_state]
        c_row = act[:, d_inner + gn + g * d_state:d_inner + gn + (g + 1) * d_state]
        st = st_ref[0, rows, :] * _to_column(decay_x[:, rows], eye) + _to_column(xd[:, rows], eye) * b_row
        st_out_ref[0, rows, :] = st
        y_tiles.append(_to_row(jnp.sum(st * c_row, axis=-1, keepdims=True), eye))
    y = jnp.concatenate(y_tiles, axis=-1) + dskip_ref[...] * xs
    gated = y * _silu(z_ref[0])
    wn = wn_ref[...]
    outs = []
    for g in range(n_groups):
        gs = slice(g * gw, (g + 1) * gw)
        outs.append(_rms(gated[:, gs], NORM_EPS) * wn[:, gs])
    o_ref[0] = jnp.concatenate(outs, axis=-1).astype(o_ref.dtype)


def _sample_ssm(xbc, z, dt, state_conv, state_ssm, conv_w, conv_b, dt_bias_p, a_log_p, dskip_x, w_norm, expand, *,
                d_state, n_groups, n_heads, headdim):
    db, conv_dim = xbc.shape
    d_inner = n_heads * headdim
    cw = conv_w.shape[0]
    per_b = lambda *tail: pl.BlockSpec((1,) + tail, lambda b: (b,) + (0,) * len(tail))
    const = lambda shape: pl.BlockSpec(shape, lambda b: (0, 0))
    o, st = pl.pallas_call(
        functools.partial(_ssm_step_body, conv_w=cw, d_inner=d_inner, d_state=d_state, n_groups=n_groups,
                          heads_per_group=n_heads // n_groups, headdim=headdim),
        grid=(db,),
        in_specs=[
            per_b(1, conv_dim), per_b(1, d_inner), per_b(1, LANES), per_b(cw - 1, conv_dim),
            per_b(d_inner, d_state),
            const((cw, conv_dim)), const((1, conv_dim)), const((1, LANES)), const((1, LANES)),
            const((1, d_inner)), const((1, d_inner)), const((LANES, d_inner)),
        ],
        out_specs=[per_b(1, d_inner), per_b(d_inner, d_state)],
        out_shape=[jax.ShapeDtypeStruct((db, 1, d_inner), BF16),
                   jax.ShapeDtypeStruct((db, d_inner, d_state), F32)],
        compiler_params=_params("arbitrary"),
        name="sample_ssm_step",
    )(xbc.reshape(db, 1, conv_dim), z.reshape(db, 1, d_inner), dt.reshape(db, 1, LANES), state_conv,
      state_ssm.reshape(db, d_inner, d_state), conv_w, conv_b, dt_bias_p, a_log_p, dskip_x, w_norm, expand)
    return o.reshape(db, d_inner), st


def _pad_lanes(v):
    return jnp.pad(v.astype(F32), (0, LANES - v.shape[0])).reshape(1, LANES)


def kernel(x_prompt, x_sample, p_prompt, p_sample, cache_k, cache_v, page_table, state_ssm, state_conv, g_mix, w_in, lambda_q1, lambda_k1, lambda_q2, lambda_k2, w_subln, conv_w, conv_b, dt_bias, a_log, d_skip, w_ssm_norm, w_out, g_ffn, w_up, w_down, w_ple, w_ple_norm, w_ple_gate, g_final):
    batch, seq, d_model = x_prompt.shape
    db, dec_seq, _ = x_sample.shape
    assert dec_seq == 1, "the sample kernels handle one new token per sequence"
    depth = w_in.shape[0]
    n_heads_a = cache_k.shape[3]
    dv = cache_v.shape[4]
    dk = lambda_q1.shape[1]
    assert cache_k.shape[4] == 2 * dk == dv == LANES
    aw = n_heads_a * dv
    n_heads_s, headdim, d_state = state_ssm.shape[2:]
    d_inner = n_heads_s * headdim
    conv_dim = state_conv.shape[3]
    n_groups = (conv_dim - d_inner) // (2 * d_state)
    assert n_heads_s <= LANES and LANES % headdim == 0 and d_state == LANES
    n_main = 3 * aw + d_inner + conv_dim
    q_scale = float(dk) ** -0.5 * LOG2E
    expand = _head_expand(n_heads_s, headdim)

    hp = x_prompt.reshape(batch * seq, d_model)
    hs = x_sample.reshape(db, d_model)
    outs = {name: [] for name in ("kp", "vp", "sp", "cp", "ks", "vs", "ss", "cs")}
    for i in range(depth):
        lam_init = 0.8 - 0.6 * math.exp(-0.3 * i)
        last = i == depth - 1
        w_main = w_in[i].astype(BF16)
        w_dt = jnp.pad(w_in[i][:, n_main:].astype(BF16), ((0, 0), (0, LANES - n_heads_s)))
        w_out_i = w_out[i].astype(BF16)
        w_up_i, w_down_i = w_up[i].astype(BF16), w_down[i].astype(BF16)
        w_ple_i, w_gate_i = w_ple[i].astype(BF16), w_ple_gate[i].astype(BF16)
        g_mix_i, g_ffn_i = g_mix[i].reshape(1, -1), g_ffn[i].reshape(1, -1)
        lams = tuple(v[i].reshape(1, dk) for v in (lambda_q1, lambda_k1, lambda_q2, lambda_k2))
        w_subln_i = w_subln[i].reshape(1, dv)
        conv_b_i = conv_b[i].reshape(1, conv_dim)
        dt_bias_p, a_log_p = _pad_lanes(dt_bias[i]), _pad_lanes(a_log[i])
        dskip_x = jnp.repeat(d_skip[i].astype(F32), headdim).reshape(1, d_inner)
        w_norm_i = w_ssm_norm[i].reshape(1, d_inner)
        w_ple_norm_i = w_ple_norm[i].reshape(1, d_model)
        g_final_r = g_final.reshape(1, d_model)
        proj = functools.partial(_inproj, g=g_mix_i, w_main=w_main, w_dt=w_dt, widths=(aw, d_inner, conv_dim),
                                 q_scale=q_scale)
        ssm_kw = dict(d_state=d_state, n_groups=n_groups, n_heads=n_heads_s, headdim=headdim)

        ple = functools.partial(_ple, w_ple=w_ple_i, w_ple_norm=w_ple_norm_i, w_gate=w_gate_i, g_final=g_final_r,
                                final_norm=last)

        q_bf, k32, v32, k_bf, v_bf, z, xbc, dt = proj(hp)
        o_attn = _prompt_attention(q_bf, k_bf, v_bf, lams, w_subln_i, batch=batch, seq=seq, n_heads=n_heads_a,
                                   dk=dk, dv=dv, lam_init=lam_init)
        o_ssm, st = _prompt_ssd(xbc, z, dt, conv_w[i], conv_b_i, dt_bias_p, a_log_p, dskip_x, w_norm_i, expand,
                                batch=batch, seq=seq, **ssm_kw)
        h1, u = _outproj(o_attn, o_ssm, hp, w_out_i, g_ffn_i)
        outs["kp"].append(k32.reshape(batch, seq, n_heads_a, 2 * dk))
        outs["vp"].append(v32.reshape(batch, seq, n_heads_a, dv))
        outs["sp"].append(st.reshape(batch, n_heads_s, headdim, d_state))
        cw = conv_w.shape[1]
        outs["cp"].append(xbc.reshape(batch, seq, conv_dim)[:, seq - (cw - 1):])
        qs_bf, ks32, vs32, _, _, zs, xbcs, dts = proj(hs)

        mlp_out, os_attn = _mlp_with_paged_attention(u, w_up_i, w_down_i, qs_bf, ks32, vs32, cache_k, cache_v,
                                                     page_table, lams, w_subln_i, layer=i, n_heads=n_heads_a, dk=dk,
                                                     dv=dv, lam_init=lam_init)
        hp = ple(h1, mlp_out, p_prompt[i].reshape(batch * seq, -1))

        os_ssm, st = _sample_ssm(xbcs, zs, dts, state_conv[i], state_ssm[i], conv_w[i], conv_b_i, dt_bias_p, a_log_p,
                                 dskip_x, w_norm_i, expand, **ssm_kw)
        h1, u = _outproj(os_attn, os_ssm, hs, w_out_i, g_ffn_i)
        hs = ple(h1, _mlp(u, w_up_i, w_down_i), p_sample[i].reshape(db, -1))
        outs["ks"].append(ks32.reshape(db, 1, n_heads_a, 2 * dk))
        outs["vs"].append(vs32.reshape(db, 1, n_heads_a, dv))
        outs["ss"].append(st.reshape(db, n_heads_s, headdim, d_state))
        outs["cs"].append(jnp.concatenate([state_conv[i][:, 1:], xbcs[:, None, :]], axis=1))

    stack = lambda name: jnp.stack(outs[name])
    return (hp.reshape(batch, seq, d_model), hs.reshape(db, 1, d_model), stack("kp"), stack("vp"), stack("sp"),
            stack("cp"), stack("ks"), stack("vs"), stack("ss"), stack("cs"))
```

```python
import functools
import math

import jax
import jax.numpy as jnp
import numpy as np
from jax import lax
from jax.experimental import pallas as pl
from jax.experimental.pallas import tpu as pltpu

F32 = jnp.float32
BF16 = jnp.bfloat16
NORM_EPS = 1e-6
SUBLN_EPS = 1e-5
HIGHEST = lax.Precision.HIGHEST
LOG2E = math.log2(math.e)

LANES = 128
SUBLANES = 8
VMEM_LIMIT_BYTES = 56 * 1024 * 1024


def _params(*semantics):
    return pltpu.CompilerParams(dimension_semantics=semantics, vmem_limit_bytes=VMEM_LIMIT_BYTES)


def _rms(x, eps):
    return x * lax.rsqrt(jnp.mean(x * x, axis=-1, keepdims=True) + eps)


def _silu(x):
    h = 0.5 * x
    return h + h * jnp.tanh(h)


def _softplus(x):
    return jnp.maximum(x, 0.0) + jnp.log1p(jnp.exp(-jnp.abs(x)))


def _row_tile(rows, target):
    t = min(rows, target)
    assert rows % t == 0, (rows, t)
    return t


def _inproj_body(x_ref, xs_ref, g_ref, w_ref, wdt_ref, q_ref, k32_ref, v32_ref, kbf_ref, vbf_ref, z_ref, xbc_ref,
                 dt_ref, qs_ref, ks_ref, vs_ref, zs_ref, xbcs_ref, dts_ref, u_ref, *, bounds, q_scale):
    j = pl.program_id(1)
    tm = x_ref.shape[0]
    ns = xs_ref.shape[0]
    sample = slice(tm, tm + ns)

    @pl.when(j == 0)
    def _():
        g = g_ref[...]
        u_ref[0:tm, :] = (_rms(x_ref[...], NORM_EPS) * g).astype(BF16)
        us = _rms(xs_ref[...], NORM_EPS) * g
        pad = jnp.zeros((u_ref.shape[0] - tm - ns, us.shape[1]), F32)
        u_ref[tm:, :] = jnp.concatenate([us, pad], axis=0).astype(BF16)
        res = jnp.dot(u_ref[...], wdt_ref[...], preferred_element_type=F32)
        dt_ref[...] = res[0:tm]
        dts_ref[...] = res[sample]

    def tile():
        return jnp.dot(u_ref[...], w_ref[...], preferred_element_type=F32)

    b_q, b_k, b_v, b_z = bounds

    @pl.when(j < b_q)
    def _():
        res = (tile() * q_scale).astype(BF16)
        q_ref[...] = res[0:tm]
        qs_ref[...] = res[sample]

    @pl.when((j >= b_q) & (j < b_k))
    def _():
        res = tile()
        k32_ref[...] = res[0:tm]
        kbf_ref[...] = res[0:tm].astype(BF16)
        ks_ref[...] = res[sample]

    @pl.when((j >= b_k) & (j < b_v))
    def _():
        res = tile()
        v32_ref[...] = res[0:tm]
        vbf_ref[...] = res[0:tm].astype(BF16)
        vs_ref[...] = res[sample]

    @pl.when((j >= b_v) & (j < b_z))
    def _():
        res = tile()
        z_ref[...] = res[0:tm]
        zs_ref[...] = res[sample]

    @pl.when(j >= b_z)
    def _():
        res = tile()
        xbc_ref[...] = res[0:tm]
        xbcs_ref[...] = res[sample]


def _inproj(x, xs, g, w_main, w_dt, *, widths, q_scale, tm_target=1024, tn=512):
    rows, d = x.shape
    ns = xs.shape[0]
    assert ns % SUBLANES == 0 and ns <= 2 * SUBLANES
    tm = _row_tile(rows, tm_target)
    aw, d_inner, conv_dim = widths
    seg = [aw, aw, aw, d_inner, conv_dim]
    assert all(s % tn == 0 for s in seg)
    tiles = [s // tn for s in seg]
    starts = [sum(tiles[:i]) for i in range(len(tiles))]
    n_tiles = sum(tiles)
    bounds = tuple(starts[i] + tiles[i] for i in range(4))

    def seg_map(s, row_block):
        lo, n = starts[s], tiles[s]
        return lambda i, j: (row_block(i), jnp.clip(j - lo, 0, n - 1))

    def out(s, dtype):
        return jax.ShapeDtypeStruct((rows, seg[s]), dtype), pl.BlockSpec((tm, tn), seg_map(s, lambda i: i))

    def out_s(s, dtype):
        return jax.ShapeDtypeStruct((ns, seg[s]), dtype), pl.BlockSpec((ns, tn), seg_map(s, lambda i: 0))

    outs = [out(0, BF16), out(1, F32), out(2, F32), out(1, BF16), out(2, BF16), out(3, F32), out(4, F32),
            (jax.ShapeDtypeStruct((rows, LANES), F32), pl.BlockSpec((tm, LANES), lambda i, j: (i, 0))),
            out_s(0, BF16), out_s(1, F32), out_s(2, F32), out_s(3, F32), out_s(4, F32),
            (jax.ShapeDtypeStruct((ns, LANES), F32), pl.BlockSpec((ns, LANES), lambda i, j: (0, 0)))]
    res = pl.pallas_call(
        functools.partial(_inproj_body, bounds=bounds, q_scale=q_scale),
        grid=(rows // tm, n_tiles),
        in_specs=[
            pl.BlockSpec((tm, d), lambda i, j: (i, 0)),
            pl.BlockSpec((ns, d), lambda i, j: (0, 0)),
            pl.BlockSpec((1, d), lambda i, j: (0, 0)),
            pl.BlockSpec((d, tn), lambda i, j: (0, j)),
            pl.BlockSpec((d, LANES), lambda i, j: (0, 0)),
        ],
        out_specs=[o[1] for o in outs],
        out_shape=[o[0] for o in outs],
        scratch_shapes=[pltpu.VMEM((tm + 2 * SUBLANES, d), BF16)],
        compiler_params=_params("arbitrary", "arbitrary"),
        name="norm_inproj",
    )(x, xs, g, w_main, w_dt)
    return res[:8], res[8:]


def _lambda_value(lq1_ref, lk1_ref, lq2_ref, lk2_ref, lam_init):
    s1 = jnp.sum(lq1_ref[...] * lk1_ref[...], axis=-1, keepdims=True)
    s2 = jnp.sum(lq2_ref[...] * lk2_ref[...], axis=-1, keepdims=True)
    return jnp.exp(s1) - jnp.exp(s2) + lam_init


def _attn_body(coef_ref, lq1_ref, lk1_ref, lq2_ref, lk2_ref, wsub_ref, qf_ref, kf_ref, q_ref, k_ref, v_ref, *rest,
               n_cast, tq, dk, dv, heads, lam_init):
    wide_refs = rest[:n_cast]
    o_ref = rest[n_cast]
    narrow_refs = rest[n_cast + 1:2 * n_cast + 1]
    vt_ref, q2_ref, s_ref, top_ref, m_ref, acc_ref = rest[2 * n_cast + 1:]
    for wide_ref, narrow_ref in zip(wide_refs, narrow_refs):
        narrow_ref[...] = wide_ref[...].astype(narrow_ref.dtype)

    hg = pl.program_id(1)
    qi = pl.program_id(2)
    cols2 = 2 * tq
    n_kb, tk = vt_ref.shape[1], vt_ref.shape[3]
    assert tk == tq
    dvx = vt_ref.shape[2]

    @pl.when(qi == 0)
    def _():
        for hb in range(heads):
            for kb in range(n_kb):
                for part in range(tk // LANES):
                    rows = slice(kb * tk + part * LANES, kb * tk + (part + 1) * LANES)
                    vt = v_ref[rows, hb * dv:(hb + 1) * dv].astype(F32).T
                    vt_ref[hb, kb, 0:dv, part * LANES:(part + 1) * LANES] = vt.astype(BF16)
                vt_ref[hb, kb, dv:dvx, :] = jnp.ones((dvx - dv, tk), BF16)
            q2_ref[hb, 0:2 * dk, :] = jnp.zeros((2 * dk, cols2), BF16)
            q2_ref[hb, 2 * dk:, 0:tq] = qf_ref[hb]
            q2_ref[hb, 2 * dk:, tq:cols2] = qf_ref[hb]

    for hb in range(heads):
        qt = q_ref[:, hb * 2 * dk:(hb + 1) * 2 * dk].astype(F32).T.astype(BF16)
        q2_ref[hb, 0:dk, 0:tq] = qt[0:dk]
        q2_ref[hb, dk:2 * dk, tq:cols2] = qt[dk:2 * dk]
    m_ref[...] = jnp.full(m_ref.shape, -jnp.inf, F32)
    acc_ref[...] = jnp.zeros(acc_ref.shape, F32)

    def score(hb, kb, slot):
        start = pl.multiple_of(kb * tk, tk)
        kx = jnp.concatenate([k_ref[pl.ds(start, tk), hb * 2 * dk:(hb + 1) * 2 * dk], kf_ref[hb]], axis=-1)
        s = jnp.dot(kx, q2_ref[hb], preferred_element_type=F32)
        s_ref[hb, slot] = s
        top_ref[hb, slot] = jnp.max(s, axis=0, keepdims=True)

    def accumulate(hb, kb, slot, masked):
        s = s_ref[hb, slot]
        off = -coef_ref[hg * heads + hb] * ((qi - kb) * tq).astype(F32)
        if masked:
            key = lax.broadcasted_iota(jnp.int32, (tk, cols2), 0)
            col = lax.broadcasted_iota(jnp.int32, (tk, cols2), 1)
            s = jnp.where(jnp.where(col >= tq, col - tq, col) >= key, s, -jnp.inf)
            top = jnp.max(s, axis=0, keepdims=True)
        else:
            top = top_ref[hb, slot]
        m_old = m_ref[hb]
        m_new = jnp.maximum(m_old, top + off)
        p = jnp.exp2(s - (m_new - off)).astype(BF16)
        alpha = jnp.exp2(m_old - m_new)
        acc_ref[hb] = alpha * acc_ref[hb] + jnp.dot(vt_ref[hb, kb], p, preferred_element_type=F32)
        m_ref[hb] = m_new

    def stage(kb_next, kb, slot, masked=False):
        for hb in range(heads):
            if kb_next is not None:
                score(hb, kb_next, 1 - slot)
            accumulate(hb, kb, slot, masked)

    def pair(j, carry):
        stage(2 * j + 1, 2 * j, 0)
        stage(2 * j + 2, 2 * j + 1, 1)
        return carry

    for hb in range(heads):
        score(hb, 0, 0)
    lax.fori_loop(0, qi // 2, pair, 0)

    @pl.when(qi % 2 == 0)
    def _():
        stage(None, qi, 0, masked=True)

    @pl.when(qi % 2 == 1)
    def _():
        stage(qi, qi - 1, 0)
        stage(None, qi, 1, masked=True)

    lam = _lambda_value(lq1_ref, lk1_ref, lq2_ref, lk2_ref, lam_init)
    for hb in range(heads):
        acc = acc_ref[hb]
        o = acc[0:dv] / acc[dv:dv + 1]
        o = o[:, 0:tq] - lam * o[:, tq:cols2]
        o = o * lax.rsqrt(jnp.mean(o * o, axis=0, keepdims=True) + SUBLN_EPS)
        o = o * wsub_ref[...] * (1.0 - lam_init)
        o_ref[:, hb * dv:(hb + 1) * dv] = o.T.astype(o_ref.dtype)


def _bf16_split3(v):
    hi = v.astype(BF16)
    r1 = v - hi.astype(np.float32)
    mid = r1.astype(BF16)
    lo = (r1 - mid.astype(np.float32)).astype(BF16)
    return [hi, mid, lo]


def _alibi_lanes(n_heads, t):
    slopes = np.float32(2.0) ** (np.float32(-8.0) * np.arange(1, n_heads + 1, dtype=np.float32) / np.float32(n_heads))
    coef = (slopes * np.float32(LOG2E)).astype(np.float32)
    cpos = coef[:, None] * np.arange(t, dtype=np.float32)[None, :]
    ones = [np.ones((n_heads, t), BF16)] * 3
    pad = [np.zeros((n_heads, t), BF16)] * (LANES - 6)
    q_lanes = np.stack(_bf16_split3(-cpos) + ones + pad, axis=1)
    k_lanes = np.stack(ones + _bf16_split3(cpos) + pad, axis=-1)
    return jnp.asarray(coef), jnp.asarray(q_lanes), jnp.asarray(k_lanes)


def _prompt_attention(q_bf, k_bf, v_bf, lams, w_subln, wide_weights, *, batch, seq, n_heads, dk, dv, lam_init,
                      tq=256, heads=4):
    tq = _row_tile(seq, tq)
    nq = seq // tq
    assert n_heads % heads == 0
    n_hg = n_heads // heads
    n_steps = batch * n_hg * nq
    coef, q_lanes, k_lanes = _alibi_lanes(n_heads, tq)
    ones_rows = 2 * SUBLANES
    lam_spec = pl.BlockSpec((1, dk), lambda b, h, i: (0, 0))
    qlanes_spec = pl.BlockSpec((heads, LANES, tq), lambda b, h, i: (h, 0, 0))
    klanes_spec = pl.BlockSpec((heads, tq, LANES), lambda b, h, i: (h, 0, 0))
    slab_specs = []
    for w in wide_weights:
        assert w.shape[0] % (n_steps * 2 * SUBLANES) == 0, w.shape
        slab_specs.append(pl.BlockSpec((w.shape[0] // n_steps, w.shape[1]),
                                       lambda b, h, i: ((b * n_hg + h) * nq + i, 0)))
    outs = pl.pallas_call(
        functools.partial(_attn_body, n_cast=len(wide_weights), tq=tq, dk=dk, dv=dv, heads=heads, lam_init=lam_init),
        grid=(batch, n_hg, nq),
        in_specs=[
            pl.BlockSpec(memory_space=pltpu.SMEM),
            lam_spec, lam_spec, lam_spec, lam_spec,
            pl.BlockSpec((dv, 1), lambda b, h, i: (0, 0)),
            qlanes_spec, klanes_spec,
            pl.BlockSpec((tq, heads * 2 * dk), lambda b, h, i: (b * nq + i, h)),
            pl.BlockSpec((seq, heads * 2 * dk), lambda b, h, i: (b, h)),
            pl.BlockSpec((seq, heads * dv), lambda b, h, i: (b, h)),
        ] + slab_specs,
        out_specs=[pl.BlockSpec((tq, heads * dv), lambda b, h, i: (b * nq + i, h))] + slab_specs,
        out_shape=[jax.ShapeDtypeStruct((batch * seq, n_heads * dv), BF16)]
                  + [jax.ShapeDtypeStruct(w.shape, BF16) for w in wide_weights],
        scratch_shapes=[
            pltpu.VMEM((heads, nq, dv + ones_rows, tq), BF16),
            pltpu.VMEM((heads, 2 * dk + LANES, 2 * tq), BF16),
            pltpu.VMEM((heads, 2, tq, 2 * tq), F32),
            pltpu.VMEM((heads, 2, 1, 2 * tq), F32),
            pltpu.VMEM((heads, 1, 2 * tq), F32),
            pltpu.VMEM((heads, dv + ones_rows, 2 * tq), F32),
        ],
        compiler_params=_params("arbitrary", "arbitrary", "arbitrary"),
        name="prompt_diff_attention",
    )(coef, *lams, w_subln.reshape(dv, 1), q_lanes, k_lanes, q_bf, k_bf, v_bf, *wide_weights)
    return outs[0], outs[1:]


def _ssd_body(xbc_ref, z_ref, dt_ref, cw_ref, cb_ref, dtb_ref, alog_ref, dskip_ref, wn_ref, e_ref,
              o_ref, st_out_ref, ext_ref, st_ref, y_ref, *, chunk, conv_w, d_inner, d_state, n_groups,
              heads_per_group, headdim):
    c = pl.program_id(1)
    n_chunks = pl.num_programs(1)
    pad = SUBLANES
    gw = heads_per_group * headdim

    @pl.when(c == 0)
    def _():
        ext_ref[0:pad, :] = jnp.zeros((pad, ext_ref.shape[1]), F32)
        st_ref[...] = jnp.zeros(st_ref.shape, F32)

    ext_ref[pad:pad + chunk, :] = xbc_ref[...]
    ext = ext_ref[...]
    conv = cb_ref[...] + ext[pad:pad + chunk] * cw_ref[conv_w - 1:conv_w, :]
    for j in range(conv_w - 1):
        shifted = pltpu.roll(ext, conv_w - 1 - j, axis=0)
        conv = conv + shifted[pad:pad + chunk] * cw_ref[j:j + 1, :]
    ext_ref[0:pad, :] = ext[chunk:chunk + pad]
    act = _silu(conv)

    xs = act[:, :d_inner]
    gn = n_groups * d_state
    bm = act[:, d_inner:d_inner + gn]
    cm = act[:, d_inner + gn:d_inner + 2 * gn]

    dtv = _softplus(dt_ref[...] + dtb_ref[...])
    da = dtv * (-jnp.exp(alog_ref[...]))
    r = lax.broadcasted_iota(jnp.int32, (chunk, chunk), 0)
    s = lax.broadcasted_iota(jnp.int32, (chunk, chunk), 1)
    causal = r >= s
    tri = causal.astype(F32).astype(BF16)
    a_cs = jnp.dot(jnp.concatenate([tri, tri, tri], axis=1), jnp.concatenate(_split3(da), axis=0),
                   preferred_element_type=F32)
    a_cs_t = a_cs.T

    per_head = jnp.concatenate([dtv, jnp.exp(a_cs), jnp.exp(a_cs[chunk - 1:chunk, :] - a_cs)], axis=0)
    spread = jnp.dot(jnp.concatenate(_split3(per_head), axis=1), e_ref[...], preferred_element_type=F32)
    dt_x = spread[0:chunk]
    eacs_x = spread[chunk:2 * chunk]
    xd = xs * dt_x
    xd_b = xd.astype(BF16)
    xdd_b = (xd * spread[2 * chunk:3 * chunk]).astype(BF16)
    chunk_decay_x = eacs_x[chunk - 1:chunk, :]

    lane = lax.broadcasted_iota(jnp.int32, (chunk, LANES), 1)
    heads_per_tile = LANES // headdim
    for g in range(n_groups):
        bm_g = bm[:, g * d_state:(g + 1) * d_state]
        cm_b = cm[:, g * d_state:(g + 1) * d_state].astype(BF16)
        cb = lax.dot_general(cm_b, bm_g.astype(BF16), (((1,), (1,)), ((), ())), preferred_element_type=F32)
        gs = slice(g * gw, (g + 1) * gw)
        st_g = st_ref[g]
        y_off = jnp.dot(cm_b, st_g.astype(BF16), preferred_element_type=F32) * eacs_x[:, gs]
        new = jnp.dot(bm_g.T.astype(BF16), xdd_b[:, gs], preferred_element_type=F32)
        st_ref[g] = st_g * chunk_decay_x[:, gs] + new
        for t in range(gw // LANES):
            lo = g * gw + t * LANES
            xd_t = xd_b[:, lo:lo + LANES]
            y_t = y_off[:, t * LANES:(t + 1) * LANES]
            for k in range(heads_per_tile):
                hh = g * heads_per_group + t * heads_per_tile + k
                seg = a_cs[:, hh:hh + 1] - a_cs_t[hh:hh + 1, :]
                lmat = jnp.exp(jnp.where(causal, seg, -jnp.inf))
                in_head = (lane >= k * headdim) & (lane < (k + 1) * headdim)
                xd_h = jnp.where(in_head, xd_t, jnp.zeros_like(xd_t))
                y_t = y_t + jnp.dot((cb * lmat).astype(BF16), xd_h, preferred_element_type=F32)
            y_ref[:, lo:lo + LANES] = y_t

    y = y_ref[...] + dskip_ref[...] * xs
    gated = y * _silu(z_ref[...])
    wn = wn_ref[...]
    for g in range(n_groups):
        gs = slice(g * gw, (g + 1) * gw)
        o_ref[:, gs] = (_rms(gated[:, gs], NORM_EPS) * wn[:, gs]).astype(o_ref.dtype)

    @pl.when(c == n_chunks - 1)
    def _():
        for g in range(n_groups):
            st_out_ref[0, g * gw:(g + 1) * gw, :] = st_ref[g].T


def _split3(x):
    hi = x.astype(BF16)
    r1 = x - hi.astype(F32)
    mid = r1.astype(BF16)
    lo = (r1 - mid.astype(F32)).astype(BF16)
    return [hi, mid, lo]


def _head_expand(n_heads, headdim):
    head_of_lane = jnp.arange(n_heads * headdim) // headdim
    return (jnp.arange(LANES)[:, None] == head_of_lane[None, :]).astype(F32)


def _prompt_ssd(xbc, z, dt, conv_w, conv_b, dt_bias_p, a_log_p, dskip_x, w_norm, expand, *, batch, seq,
                d_state, n_groups, n_heads, headdim, chunk=128):
    chunk = _row_tile(seq, chunk)
    nc = seq // chunk
    conv_dim = xbc.shape[1]
    d_inner = n_heads * headdim
    cw = conv_w.shape[0]
    hpg = n_heads // n_groups
    const = lambda shape: pl.BlockSpec(shape, lambda b, c: (0, 0))
    row_map = lambda b, c: (b * nc + c, 0)
    return pl.pallas_call(
        functools.partial(_ssd_body, chunk=chunk, conv_w=cw, d_inner=d_inner, d_state=d_state,
                          n_groups=n_groups, heads_per_group=hpg, headdim=headdim),
        grid=(batch, nc),
        in_specs=[
            pl.BlockSpec((chunk, conv_dim), row_map),
            pl.BlockSpec((chunk, d_inner), row_map),
            pl.BlockSpec((chunk, LANES), row_map),
            const((cw, conv_dim)), const((1, conv_dim)), const((1, LANES)), const((1, LANES)),
            const((1, d_inner)), const((1, d_inner)), const((3 * LANES, d_inner)),
        ],
        out_specs=[
            pl.BlockSpec((chunk, d_inner), row_map),
            pl.BlockSpec((1, d_inner, d_state), lambda b, c: (b, 0, 0)),
        ],
        out_shape=[
            jax.ShapeDtypeStruct((batch * seq, d_inner), BF16),
            jax.ShapeDtypeStruct((batch, d_inner, d_state), F32),
        ],
        scratch_shapes=[
            pltpu.VMEM((chunk + SUBLANES, conv_dim), F32),
            pltpu.VMEM((n_groups, d_state, hpg * headdim), F32),
            pltpu.VMEM((chunk, d_inner), F32),
        ],
        compiler_params=_params("arbitrary", "arbitrary"),
        name="prompt_ssd",
    )(xbc, z, dt, conv_w, conv_b, dt_bias_p, a_log_p, dskip_x, w_norm, jnp.tile(expand, (3, 1)).astype(BF16))


def _outproj_body(oa_ref, os_ref, x_ref, wa_ref, ws_ref, g_ref, h_ref, u_ref):
    h = x_ref[...] + jnp.dot(oa_ref[...], wa_ref[...], preferred_element_type=F32)
    h = h + jnp.dot(os_ref[...], ws_ref[...], preferred_element_type=F32)
    h_ref[...] = h
    u_ref[...] = (_rms(h, NORM_EPS) * g_ref[...]).astype(u_ref.dtype)


def _outproj(o_attn, o_ssm, x, w_out, g, *, tm_target=512):
    rows, d = x.shape
    tm = _row_tile(rows, tm_target)
    wa, ws = o_attn.shape[1], o_ssm.shape[1]
    assert wa == ws and w_out.shape[0] == wa + ws
    row = lambda w: pl.BlockSpec((tm, w), lambda i: (i, 0))
    return pl.pallas_call(
        _outproj_body,
        grid=(rows // tm,),
        in_specs=[row(wa), row(ws), row(d),
                  pl.BlockSpec((wa, d), lambda i: (0, 0)), pl.BlockSpec((ws, d), lambda i: (1, 0)),
                  pl.BlockSpec((1, d), lambda i: (0, 0))],
        out_specs=[row(d), row(d)],
        out_shape=[jax.ShapeDtypeStruct((rows, d), F32), jax.ShapeDtypeStruct((rows, d), BF16)],
        compiler_params=_params("arbitrary"),
        name="outproj_residual_norm",
    )(o_attn, o_ssm, x, w_out, w_out, g)


def _mlp_accumulate(u_ref, wu_ref, wd_ref, o_ref):
    a = jnp.dot(u_ref[...], wu_ref[...], preferred_element_type=F32)
    a = jnp.square(jnp.maximum(a, 0.0)).astype(BF16)
    o_ref[...] += jnp.dot(a, wd_ref[...], preferred_element_type=F32)


def _mlp_step(u_ref, wu_ref, wd_ref, o_ref):
    @pl.when(pl.program_id(1) == 0)
    def _():
        o_ref[...] = jnp.zeros(o_ref.shape, F32)

    _mlp_accumulate(u_ref, wu_ref, wd_ref, o_ref)


def _mlp(u, w_up, w_down, *, tm_target=1024, tf=512):
    rows, d = u.shape
    d_ff = w_up.shape[1]
    tm = _row_tile(rows, tm_target)
    assert d_ff % tf == 0
    return pl.pallas_call(
        _mlp_step,
        grid=(rows // tm, d_ff // tf),
        in_specs=[
            pl.BlockSpec((tm, d), lambda i, f: (i, 0)),
            pl.BlockSpec((d, tf), lambda i, f: (0, f)),
            pl.BlockSpec((tf, d), lambda i, f: (f, 0)),
        ],
        out_specs=pl.BlockSpec((tm, d), lambda i, f: (i, 0)),
        out_shape=jax.ShapeDtypeStruct((rows, d), F32),
        compiler_params=_params("arbitrary", "arbitrary"),
        name="relu2_mlp",
    )(u, w_up, w_down)


def _ple_body(h_ref, m_ref, p_ref, wp_ref, wpn_ref, wg_ref, gf_ref, o_ref, *, final_norm):
    h = h_ref[...] + m_ref[...]
    e = jnp.dot(p_ref[...].astype(BF16), wp_ref[...], preferred_element_type=F32)
    e = _rms(e, NORM_EPS) * wpn_ref[...]
    gate = jax.nn.sigmoid(jnp.dot(h.astype(BF16), wg_ref[...], preferred_element_type=F32))
    h = h + e * gate
    if final_norm:
        h = _rms(h, NORM_EPS) * gf_ref[...]
    o_ref[...] = h


def _ple(h, mlp_out, p, w_ple, w_ple_norm, w_gate, g_final, *, final_norm, tm_target=512):
    rows, d = h.shape
    pd = p.shape[1]
    tm = _row_tile(rows, tm_target)
    row = lambda w: pl.BlockSpec((tm, w), lambda i: (i, 0))
    const = lambda shape: pl.BlockSpec(shape, lambda i: (0, 0))
    return pl.pallas_call(
        functools.partial(_ple_body, final_norm=final_norm),
        grid=(rows // tm,),
        in_specs=[row(d), row(d), row(pd), const((pd, d)), const((1, d)), const((d, d)), const((1, d))],
        out_specs=row(d),
        out_shape=jax.ShapeDtypeStruct((rows, d), F32),
        compiler_params=_params("arbitrary"),
        name="ple_gate",
    )(h, mlp_out, p, w_ple, w_ple_norm, w_gate, g_final)


def _mlp_paged_body(pt_ref, u_ref, wu_ref, wd_ref, lq1_ref, lk1_ref, lq2_ref, lk2_ref, wsub_ref, q_ref, kn_ref, vn_ref,
                    *rest, pages_per_step, n_heads, dk, page, lam_init):
    del pt_ref
    k_refs = rest[:pages_per_step]
    v_refs = rest[pages_per_step:2 * pages_per_step]
    o_ref, oa_ref, qm_ref, base_ref, m_ref, l_ref, acc_ref = rest[2 * pages_per_step:]
    g = pl.program_id(1)
    n_rows = 2 * n_heads
    n_cols = page * n_heads
    head_bits = int(math.log2(n_heads))
    past_len = pl.num_programs(1) * pages_per_step * page

    rowi = lax.broadcasted_iota(jnp.int32, (n_rows, 1), 0)
    head = rowi & (n_heads - 1)
    coef = jnp.exp2(-8.0 * (head + 1).astype(F32) / n_heads) * LOG2E

    @pl.when(g == 0)
    def _():
        o_ref[...] = jnp.zeros(o_ref.shape, F32)
        q = q_ref[0].astype(F32)
        lane = lax.broadcasted_iota(jnp.int32, q.shape, 1)
        qm = jnp.concatenate([jnp.where(lane < dk, q, 0.0), jnp.where(lane >= dk, q, 0.0)], axis=0)
        qm_ref[...] = qm.astype(BF16)
        col = lax.broadcasted_iota(jnp.int32, (n_rows, n_cols), 1)
        own = (col & (n_heads - 1)) == head
        tok = lax.shift_right_logical(col, head_bits)
        dist = (past_len - tok).astype(F32)
        base_ref[...] = jnp.where(own, -coef * dist, -jnp.inf)
        m_ref[...] = jnp.full(m_ref.shape, -jnp.inf, F32)
        l_ref[...] = jnp.zeros(l_ref.shape, F32)
        acc_ref[...] = jnp.zeros(acc_ref.shape, F32)

    qm = qm_ref[...]
    scores, offs, top = [], [], None
    for r in range(pages_per_step):
        off = coef * ((g * pages_per_step + r) * page).astype(F32)
        kp = k_refs[r][...].astype(BF16)
        s = lax.dot_general(qm, kp, (((1,), (1,)), ((), ())), preferred_element_type=F32) + base_ref[...]
        cand = jnp.max(s, axis=-1, keepdims=True) + off
        top = cand if top is None else jnp.maximum(top, cand)
        scores.append(s)
        offs.append(off)
    m_old = m_ref[...]
    m_new = jnp.maximum(m_old, top)
    alpha = jnp.exp2(m_old - m_new)
    l_new = alpha * l_ref[...]
    acc_new = alpha * acc_ref[...]
    for r in range(pages_per_step):
        p = jnp.exp2(scores[r] - (m_new - offs[r]))
        l_new = l_new + jnp.sum(p, axis=-1, keepdims=True)
        acc_new = acc_new + jnp.dot(p.astype(BF16), v_refs[r][...].astype(BF16), preferred_element_type=F32)
    l_ref[...] = l_new
    acc_ref[...] = acc_new
    m_ref[...] = m_new

    _mlp_accumulate(u_ref, wu_ref, wd_ref, o_ref)

    @pl.when(g == pl.num_programs(1) - 1)
    def _():
        kn = jnp.concatenate([kn_ref[0], kn_ref[0]], axis=0)
        vn = jnp.concatenate([vn_ref[0], vn_ref[0]], axis=0)
        s_new = jnp.sum(qm_ref[...].astype(F32) * kn, axis=-1, keepdims=True)
        m_old = m_ref[...]
        m_new = jnp.maximum(m_old, s_new)
        p_new = jnp.exp2(s_new - m_new)
        alpha = jnp.exp2(m_old - m_new)
        o_all = (alpha * acc_ref[...] + p_new * vn) / (alpha * l_ref[...] + p_new)
        lam = _lambda_value(lq1_ref, lk1_ref, lq2_ref, lk2_ref, lam_init)
        o = o_all[0:n_heads] - lam * o_all[n_heads:n_rows]
        oa_ref[0] = (_rms(o, SUBLN_EPS) * wsub_ref[...] * (1.0 - lam_init)).astype(oa_ref.dtype)


def _mlp_with_paged_attention(u, w_up, w_down, q_s, k_new, v_new, cache_k, cache_v, page_table, lams, w_subln, *,
                              layer, n_heads, dk, dv, lam_init, tf=512):
    m_rows, d = u.shape
    d_ff = w_up.shape[1]
    db, n_pages = page_table.shape
    depth, n_pool, page, _, _ = cache_k.shape
    assert m_rows % db == 0 and d_ff % tf == 0
    tm, n_groups = m_rows // db, d_ff // tf
    assert n_pages % n_groups == 0 and n_heads & (n_heads - 1) == 0
    pages_per_step = n_pages // n_groups
    rows = page * n_heads
    ck = cache_k.reshape(depth * n_pool * rows, 2 * dk)
    cv = cache_v.reshape(depth * n_pool * rows, dv)
    base = layer * n_pool

    def page_spec(r, width):
        return pl.BlockSpec((rows, width), lambda b, g, pt: (base + pt[b * n_pages + g * pages_per_step + r], 0))

    lam_spec = pl.BlockSpec((1, dk), lambda b, g, pt: (0, 0))
    tok_spec = pl.BlockSpec((1, n_heads, dv), lambda b, g, pt: (b, 0, 0))
    n_rows = 2 * n_heads
    grid_spec = pltpu.PrefetchScalarGridSpec(
        num_scalar_prefetch=1,
        grid=(db, n_groups),
        in_specs=[pl.BlockSpec((tm, d), lambda b, g, pt: (b, 0)),
                  pl.BlockSpec((d, tf), lambda b, g, pt: (0, g)),
                  pl.BlockSpec((tf, d), lambda b, g, pt: (g, 0)),
                  lam_spec, lam_spec, lam_spec, lam_spec,
                  pl.BlockSpec((1, dv), lambda b, g, pt: (0, 0)),
                  tok_spec, tok_spec, tok_spec]
                 + [page_spec(r, 2 * dk) for r in range(pages_per_step)]
                 + [page_spec(r, dv) for r in range(pages_per_step)],
        out_specs=[pl.BlockSpec((tm, d), lambda b, g, pt: (b, 0)), tok_spec],
        scratch_shapes=[
            pltpu.VMEM((n_rows, 2 * dk), BF16),
            pltpu.VMEM((n_rows, rows), F32),
            pltpu.VMEM((n_rows, 1), F32),
            pltpu.VMEM((n_rows, 1), F32),
            pltpu.VMEM((n_rows, dv), F32),
        ],
    )
    heads = lambda a: a.reshape(db, n_heads, dv)
    mlp_out, o_attn = pl.pallas_call(
        functools.partial(_mlp_paged_body, pages_per_step=pages_per_step, n_heads=n_heads, dk=dk, page=page,
                          lam_init=lam_init),
        grid_spec=grid_spec,
        out_shape=[jax.ShapeDtypeStruct((m_rows, d), F32), jax.ShapeDtypeStruct((db, n_heads, dv), BF16)],
        compiler_params=_params("arbitrary", "arbitrary"),
        name="relu2_mlp_with_sample_paged_attention",
    )(page_table.reshape(-1), u, w_up, w_down, *lams, w_subln, heads(q_s), heads(k_new), heads(v_new),
      *([ck] * pages_per_step), *([cv] * pages_per_step))
    return mlp_out, o_attn.reshape(db, n_heads * dv)


def _to_column(row_vec, eye):
    n = eye.shape[0]
    return jnp.sum(jnp.where(eye, jnp.broadcast_to(row_vec, (n, n)), 0.0), axis=-1, keepdims=True)


def _to_row(col_vec, eye):
    n = eye.shape[0]
    return jnp.sum(jnp.where(eye, jnp.broadcast_to(col_vec, (n, n)), 0.0), axis=0, keepdims=True)


def _ssm_step_body(xbc_ref, z_ref, dt_ref, sc_ref, st_ref, cw_ref, cb_ref, dtb_ref, alog_ref, dskip_ref, wn_ref,
                   e_ref, o_ref, st_out_ref, *, conv_w, d_inner, d_state, n_groups, heads_per_group, headdim):
    gw = heads_per_group * headdim
    conv = cb_ref[...] + xbc_ref[0] * cw_ref[conv_w - 1:conv_w, :]
    sc = sc_ref[0]
    for j in range(conv_w - 1):
        conv = conv + sc[j:j + 1, :] * cw_ref[j:j + 1, :]
    act = _silu(conv)
    xs = act[:, :d_inner]
    gn = n_groups * d_state

    dtv = _softplus(dt_ref[0] + dtb_ref[...])
    da = dtv * (-jnp.exp(alog_ref[...]))
    expand = e_ref[...]
    dt_x = jnp.dot(dtv, expand, precision=HIGHEST, preferred_element_type=F32)
    decay_x = jnp.exp(jnp.dot(da, expand, precision=HIGHEST, preferred_element_type=F32))
    xd = xs * dt_x

    eye = (lax.broadcasted_iota(jnp.int32, (LANES, LANES), 0)
           == lax.broadcasted_iota(jnp.int32, (LANES, LANES), 1))
    y_tiles = []
    for t in range(d_inner // LANES):
        g = (t * LANES) // gw
        rows = slice(t * LANES, (t + 1) * LANES)
        b_row = act[:, d_inner + g * d_state:d_inner + (g + 1) * d_state]
        c_row = act[:, d_inner + gn + g * d_state:d_inner + gn + (g + 1) * d_state]
        st = st_ref[0, rows, :] * _to_column(decay_x[:, rows], eye) + _to_column(xd[:, rows], eye) * b_row
        st_out_ref[0, rows, :] = st
        y_tiles.append(_to_row(jnp.sum(st * c_row, axis=-1, keepdims=True), eye))
    y = jnp.concatenate(y_tiles, axis=-1) + dskip_ref[...] * xs
    gated = y * _silu(z_ref[0])
    wn = wn_ref[...]
    outs = []
    for g in range(n_groups):
        gs = slice(g * gw, (g + 1) * gw)
        outs.append(_rms(gated[:, gs], NORM_EPS) * wn[:, gs])
    o_ref[0] = jnp.concatenate(outs, axis=-1).astype(o_ref.dtype)


def _sample_ssm(xbc, z, dt, state_conv, state_ssm, conv_w, conv_b, dt_bias_p, a_log_p, dskip_x, w_norm, expand, *,
                d_state, n_groups, n_heads, headdim):
    db, conv_dim = xbc.shape
    d_inner = n_heads * headdim
    cw = conv_w.shape[0]
    per_b = lambda *tail: pl.BlockSpec((1,) + tail, lambda b: (b,) + (0,) * len(tail))
    const = lambda shape: pl.BlockSpec(shape, lambda b: (0, 0))
    o, st = pl.pallas_call(
        functools.partial(_ssm_step_body, conv_w=cw, d_inner=d_inner, d_state=d_state, n_groups=n_groups,
                          heads_per_group=n_heads // n_groups, headdim=headdim),
        grid=(db,),
        in_specs=[
            per_b(1, conv_dim), per_b(1, d_inner), per_b(1, LANES), per_b(cw - 1, conv_dim),
            per_b(d_inner, d_state),
            const((cw, conv_dim)), const((1, conv_dim)), const((1, LANES)), const((1, LANES)),
            const((1, d_inner)), const((1, d_inner)), const((LANES, d_inner)),
        ],
        out_specs=[per_b(1, d_inner), per_b(d_inner, d_state)],
        out_shape=[jax.ShapeDtypeStruct((db, 1, d_inner), BF16),
                   jax.ShapeDtypeStruct((db, d_inner, d_state), F32)],
        compiler_params=_params("arbitrary"),
        name="sample_ssm_step",
    )(xbc.reshape(db, 1, conv_dim), z.reshape(db, 1, d_inner), dt.reshape(db, 1, LANES), state_conv,
      state_ssm.reshape(db, d_inner, d_state), conv_w, conv_b, dt_bias_p, a_log_p, dskip_x, w_norm, expand)
    return o.reshape(db, d_inner), st


def _pad_lanes(v):
    return jnp.pad(v.astype(F32), (0, LANES - v.shape[0])).reshape(1, LANES)


def kernel(x_prompt, x_sample, p_prompt, p_sample, cache_k, cache_v, page_table, state_ssm, state_conv, g_mix, w_in, lambda_q1, lambda_k1, lambda_q2, lambda_k2, w_subln, conv_w, conv_b, dt_bias, a_log, d_skip, w_ssm_norm, w_out, g_ffn, w_up, w_down, w_ple, w_ple_norm, w_ple_gate, g_final):
    batch, seq, d_model = x_prompt.shape
    db, dec_seq, _ = x_sample.shape
    assert dec_seq == 1, "the sample kernels handle one new token per sequence"
    depth = w_in.shape[0]
    n_heads_a = cache_k.shape[3]
    dv = cache_v.shape[4]
    dk = lambda_q1.shape[1]
    assert cache_k.shape[4] == 2 * dk == dv == LANES
    aw = n_heads_a * dv
    n_heads_s, headdim, d_state = state_ssm.shape[2:]
    d_inner = n_heads_s * headdim
    conv_dim = state_conv.shape[3]
    n_groups = (conv_dim - d_inner) // (2 * d_state)
    assert n_heads_s <= LANES and LANES % headdim == 0 and d_state == LANES
    n_main = 3 * aw + d_inner + conv_dim
    q_scale = float(dk) ** -0.5 * LOG2E
    expand = _head_expand(n_heads_s, headdim)

    hp = x_prompt.reshape(batch * seq, d_model)
    hs = x_sample.reshape(db, d_model)
    outs = {name: [] for name in ("kp", "vp", "sp", "cp", "ks", "vs", "ss", "cs")}
    for i in range(depth):
        lam_init = 0.8 - 0.6 * math.exp(-0.3 * i)
        last = i == depth - 1
        w_main = w_in[i].astype(BF16)
        w_dt = jnp.pad(w_in[i][:, n_main:].astype(BF16), ((0, 0), (0, LANES - n_heads_s)))
        w_ple_i = w_ple[i].astype(BF16)
        g_mix_i, g_ffn_i = g_mix[i].reshape(1, -1), g_ffn[i].reshape(1, -1)
        lams = tuple(v[i].reshape(1, dk) for v in (lambda_q1, lambda_k1, lambda_q2, lambda_k2))
        w_subln_i = w_subln[i].reshape(1, dv)
        conv_b_i = conv_b[i].reshape(1, conv_dim)
        dt_bias_p, a_log_p = _pad_lanes(dt_bias[i]), _pad_lanes(a_log[i])
        dskip_x = jnp.repeat(d_skip[i].astype(F32), headdim).reshape(1, d_inner)
        w_norm_i = w_ssm_norm[i].reshape(1, d_inner)
        w_ple_norm_i = w_ple_norm[i].reshape(1, d_model)
        g_final_r = g_final.reshape(1, d_model)
        proj = functools.partial(_inproj, g=g_mix_i, w_main=w_main, w_dt=w_dt, widths=(aw, d_inner, conv_dim),
                                 q_scale=q_scale)
        ssm_kw = dict(d_state=d_state, n_groups=n_groups, n_heads=n_heads_s, headdim=headdim)

        (q_bf, k32, v32, k_bf, v_bf, z, xbc, dt), (qs_bf, ks32, vs32, zs, xbcs, dts) = proj(hp, hs)
        o_attn, (w_out_i, w_up_i, w_down_i, w_gate_i) = _prompt_attention(
            q_bf, k_bf, v_bf, lams, w_subln_i, (w_out[i], w_up[i], w_down[i], w_ple_gate[i]), batch=batch, seq=seq,
            n_heads=n_heads_a, dk=dk, dv=dv, lam_init=lam_init)
        ple = functools.partial(_ple, w_ple=w_ple_i, w_ple_norm=w_ple_norm_i, w_gate=w_gate_i, g_final=g_final_r,
                                final_norm=last)
        o_ssm, st = _prompt_ssd(xbc, z, dt, conv_w[i], conv_b_i, dt_bias_p, a_log_p, dskip_x, w_norm_i, expand,
                                batch=batch, seq=seq, **ssm_kw)
        h1, u = _outproj(o_attn, o_ssm, hp, w_out_i, g_ffn_i)
        outs["kp"].append(k32.reshape(batch, seq, n_heads_a, 2 * dk))
        outs["vp"].append(v32.reshape(batch, seq, n_heads_a, dv))
        outs["sp"].append(st.reshape(batch, n_heads_s, headdim, d_state))
        cw = conv_w.shape[1]
        outs["cp"].append(xbc.reshape(batch, seq, conv_dim)[:, seq - (cw - 1):])

        mlp_out, os_attn = _mlp_with_paged_attention(u, w_up_i, w_down_i, qs_bf, ks32, vs32, cache_k, cache_v,
                                                     page_table, lams, w_subln_i, layer=i, n_heads=n_heads_a, dk=dk,
                                                     dv=dv, lam_init=lam_init)
        hp = ple(h1, mlp_out, p_prompt[i].reshape(batch * seq, -1))

        os_ssm, st = _sample_ssm(xbcs, zs, dts, state_conv[i], state_ssm[i], conv_w[i], conv_b_i, dt_bias_p, a_log_p,
                                 dskip_x, w_norm_i, expand, **ssm_kw)
        h1, u = _outproj(os_attn, os_ssm, hs, w_out_i, g_ffn_i)
        hs = ple(h1, _mlp(u, w_up_i, w_down_i), p_sample[i].reshape(db, -1))
        outs["ks"].append(ks32.reshape(db, 1, n_heads_a, 2 * dk))
        outs["vs"].append(vs32.reshape(db, 1, n_heads_a, dv))
        outs["ss"].append(st.reshape(db, n_heads_s, headdim, d_state))
        outs["cs"].append(jnp.concatenate([state_conv[i][:, 1:], xbcs[:, None, :]], axis=1))

    stack = lambda name: jnp.stack(outs[name])
    return (hp.reshape(batch, seq, d_model), hs.reshape(db, 1, d_model), stack("kp"), stack("vp"), stack("sp"),
            stack("cp"), stack("ks"), stack("vs"), stack("ss"), stack("cs"))
```

```python
import functools
import math

import jax
import jax.numpy as jnp
import numpy as np
from jax import lax
from jax.experimental import pallas as pl
from jax.experimental.pallas import tpu as pltpu

F32 = jnp.float32
BF16 = jnp.bfloat16
NORM_EPS = 1e-6
SUBLN_EPS = 1e-5
HIGHEST = lax.Precision.HIGHEST
LOG2E = math.log2(math.e)

LANES = 128
SUBLANES = 8
VMEM_LIMIT_BYTES = 56 * 1024 * 1024


def _params(*semantics):
    return pltpu.CompilerParams(dimension_semantics=semantics, vmem_limit_bytes=VMEM_LIMIT_BYTES)


def _rms(x, eps):
    return x * lax.rsqrt(jnp.mean(x * x, axis=-1, keepdims=True) + eps)


def _silu(x):
    h = 0.5 * x
    return h + h * jnp.tanh(h)


def _softplus(x):
    return jnp.maximum(x, 0.0) + jnp.log1p(jnp.exp(-jnp.abs(x)))


def _row_tile(rows, target):
    t = min(rows, target)
    assert rows % t == 0, (rows, t)
    return t


def _inproj_body(x_ref, xs_ref, g_ref, w_ref, wdt_ref, q_ref, k32_ref, v32_ref, kbf_ref, vbf_ref, z_ref, xbc_ref,
                 dt_ref, qs_ref, ks_ref, vs_ref, zs_ref, xbcs_ref, dts_ref, u_ref, *, bounds, q_scale):
    j = pl.program_id(1)
    tm = x_ref.shape[0]
    ns = xs_ref.shape[0]
    sample = slice(tm, tm + ns)
    first_row_tile = pl.program_id(0) == 0

    def store_sample(ref, val):
        @pl.when(first_row_tile)
        def _():
            ref[...] = val

    @pl.when(j == 0)
    def _():
        g = g_ref[...]
        u_ref[0:tm, :] = (_rms(x_ref[...], NORM_EPS) * g).astype(BF16)
        us = _rms(xs_ref[...], NORM_EPS) * g
        pad = jnp.zeros((u_ref.shape[0] - tm - ns, us.shape[1]), F32)
        u_ref[tm:, :] = jnp.concatenate([us, pad], axis=0).astype(BF16)
        res = jnp.dot(u_ref[...], wdt_ref[...], preferred_element_type=F32)
        dt_ref[...] = res[0:tm]
        store_sample(dts_ref, res[sample])

    def tile():
        return jnp.dot(u_ref[...], w_ref[...], preferred_element_type=F32)

    b_q, b_k, b_v, b_z = bounds

    @pl.when(j < b_q)
    def _():
        res = (tile() * q_scale).astype(BF16)
        q_ref[...] = res[0:tm]
        store_sample(qs_ref, res[sample])

    @pl.when((j >= b_q) & (j < b_k))
    def _():
        res = tile()
        k32_ref[...] = res[0:tm]
        kbf_ref[...] = res[0:tm].astype(BF16)
        store_sample(ks_ref, res[sample])

    @pl.when((j >= b_k) & (j < b_v))
    def _():
        res = tile()
        v32_ref[...] = res[0:tm]
        vbf_ref[...] = res[0:tm].astype(BF16)
        store_sample(vs_ref, res[sample])

    @pl.when((j >= b_v) & (j < b_z))
    def _():
        res = tile()
        z_ref[...] = res[0:tm]
        store_sample(zs_ref, res[sample])

    @pl.when(j >= b_z)
    def _():
        res = tile()
        xbc_ref[...] = res[0:tm]
        store_sample(xbcs_ref, res[sample])


def _inproj(x, xs, g, w_main, w_dt, *, widths, q_scale, tm_target=1024, tn=512):
    rows, d = x.shape
    ns = xs.shape[0]
    assert ns % SUBLANES == 0 and ns <= 2 * SUBLANES
    tm = _row_tile(rows, tm_target)
    aw, d_inner, conv_dim = widths
    seg = [aw, aw, aw, d_inner, conv_dim]
    assert all(s % tn == 0 for s in seg)
    tiles = [s // tn for s in seg]
    starts = [sum(tiles[:i]) for i in range(len(tiles))]
    n_tiles = sum(tiles)
    bounds = tuple(starts[i] + tiles[i] for i in range(4))

    def seg_map(s):
        lo, n = starts[s], tiles[s]
        return lambda i, j: (i, jnp.clip(j - lo, 0, n - 1))

    def seg_map_first_tile_only(s):
        lo, n = starts[s], tiles[s]
        return lambda i, j: (0, jnp.where(i == 0, jnp.clip(j - lo, 0, n - 1), n - 1))

    def out(s, dtype):
        return jax.ShapeDtypeStruct((rows, seg[s]), dtype), pl.BlockSpec((tm, tn), seg_map(s))

    def out_s(s, dtype):
        return jax.ShapeDtypeStruct((ns, seg[s]), dtype), pl.BlockSpec((ns, tn), seg_map_first_tile_only(s))

    outs = [out(0, BF16), out(1, F32), out(2, F32), out(1, BF16), out(2, BF16), out(3, F32), out(4, F32),
            (jax.ShapeDtypeStruct((rows, LANES), F32), pl.BlockSpec((tm, LANES), lambda i, j: (i, 0))),
            out_s(0, BF16), out_s(1, F32), out_s(2, F32), out_s(3, F32), out_s(4, F32),
            (jax.ShapeDtypeStruct((ns, LANES), F32), pl.BlockSpec((ns, LANES), lambda i, j: (0, 0)))]
    res = pl.pallas_call(
        functools.partial(_inproj_body, bounds=bounds, q_scale=q_scale),
        grid=(rows // tm, n_tiles),
        in_specs=[
            pl.BlockSpec((tm, d), lambda i, j: (i, 0)),
            pl.BlockSpec((ns, d), lambda i, j: (0, 0)),
            pl.BlockSpec((1, d), lambda i, j: (0, 0)),
            pl.BlockSpec((d, tn), lambda i, j: (0, j)),
            pl.BlockSpec((d, LANES), lambda i, j: (0, 0)),
        ],
        out_specs=[o[1] for o in outs],
        out_shape=[o[0] for o in outs],
        scratch_shapes=[pltpu.VMEM((tm + 2 * SUBLANES, d), BF16)],
        compiler_params=_params("arbitrary", "arbitrary"),
        name="norm_inproj",
    )(x, xs, g, w_main, w_dt)
    return res[:8], res[8:]


def _lambda_value(lq1_ref, lk1_ref, lq2_ref, lk2_ref, lam_init):
    s1 = jnp.sum(lq1_ref[...] * lk1_ref[...], axis=-1, keepdims=True)
    s2 = jnp.sum(lq2_ref[...] * lk2_ref[...], axis=-1, keepdims=True)
    return jnp.exp(s1) - jnp.exp(s2) + lam_init


def _attn_body(coef_ref, lq1_ref, lk1_ref, lq2_ref, lk2_ref, wsub_ref, qf_ref, kf_ref, q_ref, k_ref, v_ref, *rest,
               n_cast, tq, dk, dv, heads, lam_init):
    wide_refs = rest[:n_cast]
    o_ref = rest[n_cast]
    narrow_refs = rest[n_cast + 1:2 * n_cast + 1]
    vt_ref, q2_ref, s_ref, top_ref, m_ref, acc_ref = rest[2 * n_cast + 1:]
    for wide_ref, narrow_ref in zip(wide_refs, narrow_refs):
        narrow_ref[...] = wide_ref[...].astype(narrow_ref.dtype)

    hg = pl.program_id(1)
    qi = pl.program_id(2)
    cols2 = 2 * tq
    n_kb, tk = vt_ref.shape[1], vt_ref.shape[3]
    assert tk == tq
    dvx = vt_ref.shape[2]

    @pl.when(qi == 0)
    def _():
        for hb in range(heads):
            for kb in range(n_kb):
                for part in range(tk // LANES):
                    rows = slice(kb * tk + part * LANES, kb * tk + (part + 1) * LANES)
                    vt = v_ref[rows, hb * dv:(hb + 1) * dv].astype(F32).T
                    vt_ref[hb, kb, 0:dv, part * LANES:(part + 1) * LANES] = vt.astype(BF16)
                vt_ref[hb, kb, dv:dvx, :] = jnp.ones((dvx - dv, tk), BF16)
            q2_ref[hb, 0:2 * dk, :] = jnp.zeros((2 * dk, cols2), BF16)
            q2_ref[hb, 2 * dk:, 0:tq] = qf_ref[hb]
            q2_ref[hb, 2 * dk:, tq:cols2] = qf_ref[hb]

    for hb in range(heads):
        qt = q_ref[:, hb * 2 * dk:(hb + 1) * 2 * dk].astype(F32).T.astype(BF16)
        q2_ref[hb, 0:dk, 0:tq] = qt[0:dk]
        q2_ref[hb, dk:2 * dk, tq:cols2] = qt[dk:2 * dk]
    m_ref[...] = jnp.full(m_ref.shape, -jnp.inf, F32)
    acc_ref[...] = jnp.zeros(acc_ref.shape, F32)

    def score(hb, kb, slot):
        start = pl.multiple_of(kb * tk, tk)
        kx = jnp.concatenate([k_ref[pl.ds(start, tk), hb * 2 * dk:(hb + 1) * 2 * dk], kf_ref[hb]], axis=-1)
        s = jnp.dot(kx, q2_ref[hb], preferred_element_type=F32)
        s_ref[hb, slot] = s
        top_ref[hb, slot] = jnp.max(s, axis=0, keepdims=True)

    def accumulate(hb, kb, slot, masked):
        s = s_ref[hb, slot]
        off = -coef_ref[hg * heads + hb] * ((qi - kb) * tq).astype(F32)
        if masked:
            key = lax.broadcasted_iota(jnp.int32, (tk, cols2), 0)
            col = lax.broadcasted_iota(jnp.int32, (tk, cols2), 1)
            s = jnp.where(jnp.where(col >= tq, col - tq, col) >= key, s, -jnp.inf)
            top = jnp.max(s, axis=0, keepdims=True)
        else:
            top = top_ref[hb, slot]
        m_old = m_ref[hb]
        m_new = jnp.maximum(m_old, top + off)
        p = jnp.exp2(s - (m_new - off)).astype(BF16)
        alpha = jnp.exp2(m_old - m_new)
        acc_ref[hb] = alpha * acc_ref[hb] + jnp.dot(vt_ref[hb, kb], p, preferred_element_type=F32)
        m_ref[hb] = m_new

    def stage(kb_next, kb, slot, masked=False):
        for hb in range(heads):
            if kb_next is not None:
                score(hb, kb_next, 1 - slot)
            accumulate(hb, kb, slot, masked)

    def pair(j, carry):
        stage(2 * j + 1, 2 * j, 0)
        stage(2 * j + 2, 2 * j + 1, 1)
        return carry

    for hb in range(heads):
        score(hb, 0, 0)
    lax.fori_loop(0, qi // 2, pair, 0)

    @pl.when(qi % 2 == 0)
    def _():
        stage(None, qi, 0, masked=True)

    @pl.when(qi % 2 == 1)
    def _():
        stage(qi, qi - 1, 0)
        stage(None, qi, 1, masked=True)

    lam = _lambda_value(lq1_ref, lk1_ref, lq2_ref, lk2_ref, lam_init)
    for hb in range(heads):
        acc = acc_ref[hb]
        o = acc[0:dv] / acc[dv:dv + 1]
        o = o[:, 0:tq] - lam * o[:, tq:cols2]
        o = o * lax.rsqrt(jnp.mean(o * o, axis=0, keepdims=True) + SUBLN_EPS)
        o = o * wsub_ref[...] * (1.0 - lam_init)
        o_ref[:, hb * dv:(hb + 1) * dv] = o.T.astype(o_ref.dtype)


def _bf16_split3(v):
    hi = v.astype(BF16)
    r1 = v - hi.astype(np.float32)
    mid = r1.astype(BF16)
    lo = (r1 - mid.astype(np.float32)).astype(BF16)
    return [hi, mid, lo]


def _alibi_lanes(n_heads, t):
    slopes = np.float32(2.0) ** (np.float32(-8.0) * np.arange(1, n_heads + 1, dtype=np.float32) / np.float32(n_heads))
    coef = (slopes * np.float32(LOG2E)).astype(np.float32)
    cpos = coef[:, None] * np.arange(t, dtype=np.float32)[None, :]
    ones = [np.ones((n_heads, t), BF16)] * 3
    pad = [np.zeros((n_heads, t), BF16)] * (LANES - 6)
    q_lanes = np.stack(_bf16_split3(-cpos) + ones + pad, axis=1)
    k_lanes = np.stack(ones + _bf16_split3(cpos) + pad, axis=-1)
    return jnp.asarray(coef), jnp.asarray(q_lanes), jnp.asarray(k_lanes)


def _prompt_attention(q_bf, k_bf, v_bf, lams, w_subln, wide_weights, *, batch, seq, n_heads, dk, dv, lam_init,
                      tq=256, heads=4):
    tq = _row_tile(seq, tq)
    nq = seq // tq
    assert n_heads % heads == 0
    n_hg = n_heads // heads
    n_steps = batch * n_hg * nq
    coef, q_lanes, k_lanes = _alibi_lanes(n_heads, tq)
    ones_rows = 2 * SUBLANES
    lam_spec = pl.BlockSpec((1, dk), lambda b, h, i: (0, 0))
    qlanes_spec = pl.BlockSpec((heads, LANES, tq), lambda b, h, i: (h, 0, 0))
    klanes_spec = pl.BlockSpec((heads, tq, LANES), lambda b, h, i: (h, 0, 0))
    slab_specs = []
    for w in wide_weights:
        assert w.shape[0] % (n_steps * 2 * SUBLANES) == 0, w.shape
        slab_specs.append(pl.BlockSpec((w.shape[0] // n_steps, w.shape[1]),
                                       lambda b, h, i: ((b * n_hg + h) * nq + i, 0)))
    outs = pl.pallas_call(
        functools.partial(_attn_body, n_cast=len(wide_weights), tq=tq, dk=dk, dv=dv, heads=heads, lam_init=lam_init),
        grid=(batch, n_hg, nq),
        in_specs=[
            pl.BlockSpec(memory_space=pltpu.SMEM),
            lam_spec, lam_spec, lam_spec, lam_spec,
            pl.BlockSpec((dv, 1), lambda b, h, i: (0, 0)),
            qlanes_spec, klanes_spec,
            pl.BlockSpec((tq, heads * 2 * dk), lambda b, h, i: (b * nq + i, h)),
            pl.BlockSpec((seq, heads * 2 * dk), lambda b, h, i: (b, h)),
            pl.BlockSpec((seq, heads * dv), lambda b, h, i: (b, h)),
        ] + slab_specs,
        out_specs=[pl.BlockSpec((tq, heads * dv), lambda b, h, i: (b * nq + i, h))] + slab_specs,
        out_shape=[jax.ShapeDtypeStruct((batch * seq, n_heads * dv), BF16)]
                  + [jax.ShapeDtypeStruct(w.shape, BF16) for w in wide_weights],
        scratch_shapes=[
            pltpu.VMEM((heads, nq, dv + ones_rows, tq), BF16),
            pltpu.VMEM((heads, 2 * dk + LANES, 2 * tq), BF16),
            pltpu.VMEM((heads, 2, tq, 2 * tq), F32),
            pltpu.VMEM((heads, 2, 1, 2 * tq), F32),
            pltpu.VMEM((heads, 1, 2 * tq), F32),
            pltpu.VMEM((heads, dv + ones_rows, 2 * tq), F32),
        ],
        compiler_params=_params("arbitrary", "arbitrary", "arbitrary"),
        name="prompt_diff_attention",
    )(coef, *lams, w_subln.reshape(dv, 1), q_lanes, k_lanes, q_bf, k_bf, v_bf, *wide_weights)
    return outs[0], outs[1:]


def _ssd_body(xbc_ref, z_ref, dt_ref, cw_ref, cb_ref, dtb_ref, alog_ref, dskip_ref, wn_ref, e_ref,
              o_ref, st_out_ref, ext_ref, st_ref, y_ref, *, chunk, conv_w, d_inner, d_state, n_groups,
              heads_per_group, headdim):
    c = pl.program_id(1)
    n_chunks = pl.num_programs(1)
    pad = SUBLANES
    gw = heads_per_group * headdim

    @pl.when(c == 0)
    def _():
        ext_ref[0:pad, :] = jnp.zeros((pad, ext_ref.shape[1]), F32)
        st_ref[...] = jnp.zeros(st_ref.shape, F32)

    ext_ref[pad:pad + chunk, :] = xbc_ref[...]
    ext = ext_ref[...]
    conv = cb_ref[...] + ext[pad:pad + chunk] * cw_ref[conv_w - 1:conv_w, :]
    for j in range(conv_w - 1):
        shifted = pltpu.roll(ext, conv_w - 1 - j, axis=0)
        conv = conv + shifted[pad:pad + chunk] * cw_ref[j:j + 1, :]
    ext_ref[0:pad, :] = ext[chunk:chunk + pad]
    act = _silu(conv)

    xs = act[:, :d_inner]
    gn = n_groups * d_state
    bm = act[:, d_inner:d_inner + gn]
    cm = act[:, d_inner + gn:d_inner + 2 * gn]

    dtv = _softplus(dt_ref[...] + dtb_ref[...])
    da = dtv * (-jnp.exp(alog_ref[...]))
    r = lax.broadcasted_iota(jnp.int32, (chunk, chunk), 0)
    s = lax.broadcasted_iota(jnp.int32, (chunk, chunk), 1)
    causal = r >= s
    tri = causal.astype(F32).astype(BF16)
    a_cs = jnp.dot(jnp.concatenate([tri, tri, tri], axis=1), jnp.concatenate(_split3(da), axis=0),
                   preferred_element_type=F32)
    a_cs_t = a_cs.T

    per_head = jnp.concatenate([dtv, jnp.exp(a_cs), jnp.exp(a_cs[chunk - 1:chunk, :] - a_cs)], axis=0)
    spread = jnp.dot(jnp.concatenate(_split3(per_head), axis=1), e_ref[...], preferred_element_type=F32)
    dt_x = spread[0:chunk]
    eacs_x = spread[chunk:2 * chunk]
    xd = xs * dt_x
    xd_b = xd.astype(BF16)
    xdd_b = (xd * spread[2 * chunk:3 * chunk]).astype(BF16)
    chunk_decay_x = eacs_x[chunk - 1:chunk, :]

    lane = lax.broadcasted_iota(jnp.int32, (chunk, LANES), 1)
    heads_per_tile = LANES // headdim
    for g in range(n_groups):
        bm_g = bm[:, g * d_state:(g + 1) * d_state]
        cm_b = cm[:, g * d_state:(g + 1) * d_state].astype(BF16)
        cb = lax.dot_general(cm_b, bm_g.astype(BF16), (((1,), (1,)), ((), ())), preferred_element_type=F32)
        gs = slice(g * gw, (g + 1) * gw)
        st_g = st_ref[g]
        y_off = jnp.dot(cm_b, st_g.astype(BF16), preferred_element_type=F32) * eacs_x[:, gs]
        new = jnp.dot(bm_g.T.astype(BF16), xdd_b[:, gs], preferred_element_type=F32)
        st_ref[g] = st_g * chunk_decay_x[:, gs] + new
        for t in range(gw // LANES):
            lo = g * gw + t * LANES
            xd_t = xd_b[:, lo:lo + LANES]
            y_t = y_off[:, t * LANES:(t + 1) * LANES]
            for k in range(heads_per_tile):
                hh = g * heads_per_group + t * heads_per_tile + k
                seg = a_cs[:, hh:hh + 1] - a_cs_t[hh:hh + 1, :]
                lmat = jnp.exp(jnp.where(causal, seg, -jnp.inf))
                in_head = (lane >= k * headdim) & (lane < (k + 1) * headdim)
                xd_h = jnp.where(in_head, xd_t, jnp.zeros_like(xd_t))
                y_t = y_t + jnp.dot((cb * lmat).astype(BF16), xd_h, preferred_element_type=F32)
            y_ref[:, lo:lo + LANES] = y_t

    y = y_ref[...] + dskip_ref[...] * xs
    gated = y * _silu(z_ref[...])
    wn = wn_ref[...]
    for g in range(n_groups):
        gs = slice(g * gw, (g + 1) * gw)
        o_ref[:, gs] = (_rms(gated[:, gs], NORM_EPS) * wn[:, gs]).astype(o_ref.dtype)

    @pl.when(c == n_chunks - 1)
    def _():
        for g in range(n_groups):
            st_out_ref[0, g * gw:(g + 1) * gw, :] = st_ref[g].T


def _split3(x):
    hi = x.astype(BF16)
    r1 = x - hi.astype(F32)
    mid = r1.astype(BF16)
    lo = (r1 - mid.astype(F32)).astype(BF16)
    return [hi, mid, lo]


def _head_expand(n_heads, headdim):
    head_of_lane = jnp.arange(n_heads * headdim) // headdim
    return (jnp.arange(LANES)[:, None] == head_of_lane[None, :]).astype(F32)


def _prompt_ssd(xbc, z, dt, conv_w, conv_b, dt_bias_p, a_log_p, dskip_x, w_norm, expand, *, batch, seq,
                d_state, n_groups, n_heads, headdim, chunk=128):
    chunk = _row_tile(seq, chunk)
    nc = seq // chunk
    conv_dim = xbc.shape[1]
    d_inner = n_heads * headdim
    cw = conv_w.shape[0]
    hpg = n_heads // n_groups
    const = lambda shape: pl.BlockSpec(shape, lambda b, c: (0, 0))
    row_map = lambda b, c: (b * nc + c, 0)
    return pl.pallas_call(
        functools.partial(_ssd_body, chunk=chunk, conv_w=cw, d_inner=d_inner, d_state=d_state,
                          n_groups=n_groups, heads_per_group=hpg, headdim=headdim),
        grid=(batch, nc),
        in_specs=[
            pl.BlockSpec((chunk, conv_dim), row_map),
            pl.BlockSpec((chunk, d_inner), row_map),
            pl.BlockSpec((chunk, LANES), row_map),
            const((cw, conv_dim)), const((1, conv_dim)), const((1, LANES)), const((1, LANES)),
            const((1, d_inner)), const((1, d_inner)), const((3 * LANES, d_inner)),
        ],
        out_specs=[
            pl.BlockSpec((chunk, d_inner), row_map),
            pl.BlockSpec((1, d_inner, d_state), lambda b, c: (b, 0, 0)),
        ],
        out_shape=[
            jax.ShapeDtypeStruct((batch * seq, d_inner), BF16),
            jax.ShapeDtypeStruct((batch, d_inner, d_state), F32),
        ],
        scratch_shapes=[
            pltpu.VMEM((chunk + SUBLANES, conv_dim), F32),
            pltpu.VMEM((n_groups, d_state, hpg * headdim), F32),
            pltpu.VMEM((chunk, d_inner), F32),
        ],
        compiler_params=_params("arbitrary", "arbitrary"),
        name="prompt_ssd",
    )(xbc, z, dt, conv_w, conv_b, dt_bias_p, a_log_p, dskip_x, w_norm, jnp.tile(expand, (3, 1)).astype(BF16))


def _outproj_body(oa_ref, os_ref, x_ref, wa_ref, ws_ref, g_ref, h_ref, u_ref):
    h = x_ref[...] + jnp.dot(oa_ref[...], wa_ref[...], preferred_element_type=F32)
    h = h + jnp.dot(os_ref[...], ws_ref[...], preferred_element_type=F32)
    h_ref[...] = h
    u_ref[...] = (_rms(h, NORM_EPS) * g_ref[...]).astype(u_ref.dtype)


def _outproj(o_attn, o_ssm, x, w_out, g, *, tm_target=512):
    rows, d = x.shape
    tm = _row_tile(rows, tm_target)
    wa, ws = o_attn.shape[1], o_ssm.shape[1]
    assert wa == ws and w_out.shape[0] == wa + ws
    row = lambda w: pl.BlockSpec((tm, w), lambda i: (i, 0))
    return pl.pallas_call(
        _outproj_body,
        grid=(rows // tm,),
        in_specs=[row(wa), row(ws), row(d),
                  pl.BlockSpec((wa, d), lambda i: (0, 0)), pl.BlockSpec((ws, d), lambda i: (1, 0)),
                  pl.BlockSpec((1, d), lambda i: (0, 0))],
        out_specs=[row(d), row(d)],
        out_shape=[jax.ShapeDtypeStruct((rows, d), F32), jax.ShapeDtypeStruct((rows, d), BF16)],
        compiler_params=_params("arbitrary"),
        name="outproj_residual_norm",
    )(o_attn, o_ssm, x, w_out, w_out, g)


def _mlp_accumulate(u_ref, wu_ref, wd_ref, o_ref):
    a = jnp.dot(u_ref[...], wu_ref[...], preferred_element_type=F32)
    a = jnp.square(jnp.maximum(a, 0.0)).astype(BF16)
    o_ref[...] += jnp.dot(a, wd_ref[...], preferred_element_type=F32)


def _ple_body(h_ref, m_ref, p_ref, wp_ref, wpn_ref, wg_ref, gf_ref, *rest, final_norm, with_small_mlp):
    if with_small_mlp:
        us_ref, wu_ref, wd_ref, o_ref, ms_ref = rest

        @pl.when(pl.program_id(0) == 0)
        def _():
            ms_ref[...] = jnp.zeros(ms_ref.shape, F32)

        _mlp_accumulate(us_ref, wu_ref, wd_ref, ms_ref)
    else:
        (o_ref,) = rest
    h = h_ref[...] + m_ref[...]
    e = jnp.dot(p_ref[...].astype(BF16), wp_ref[...], preferred_element_type=F32)
    e = _rms(e, NORM_EPS) * wpn_ref[...]
    gate = jax.nn.sigmoid(jnp.dot(h.astype(BF16), wg_ref[...], preferred_element_type=F32))
    h = h + e * gate
    if final_norm:
        h = _rms(h, NORM_EPS) * gf_ref[...]
    o_ref[...] = h


def _ple(h, mlp_out, p, w_ple, w_ple_norm, w_gate, g_final, *, final_norm, small_mlp=None, tm_target=512):
    rows, d = h.shape
    pd = p.shape[1]
    tm = _row_tile(rows, tm_target)
    n_steps = rows // tm
    row = lambda w: pl.BlockSpec((tm, w), lambda i: (i, 0))
    const = lambda shape: pl.BlockSpec(shape, lambda i: (0, 0))
    in_specs = [row(d), row(d), row(pd), const((pd, d)), const((1, d)), const((d, d)), const((1, d))]
    out_specs, out_shape, extra = row(d), jax.ShapeDtypeStruct((rows, d), F32), ()
    if small_mlp is not None:
        u_small, w_up, w_down = small_mlp
        d_ff = w_up.shape[1]
        assert d_ff % (n_steps * LANES) == 0
        tf = d_ff // n_steps
        in_specs += [const(u_small.shape), pl.BlockSpec((d, tf), lambda i: (0, i)),
                     pl.BlockSpec((tf, d), lambda i: (i, 0))]
        out_specs = [out_specs, const(u_small.shape)]
        out_shape = [out_shape, jax.ShapeDtypeStruct(u_small.shape, F32)]
        extra = small_mlp
    return pl.pallas_call(
        functools.partial(_ple_body, final_norm=final_norm, with_small_mlp=small_mlp is not None),
        grid=(n_steps,),
        in_specs=in_specs,
        out_specs=out_specs,
        out_shape=out_shape,
        compiler_params=_params("arbitrary"),
        name="ple_gate",
    )(h, mlp_out, p, w_ple, w_ple_norm, w_gate, g_final, *extra)


def _mlp_paged_body(pt_ref, u_ref, wu_ref, wd_ref, lq1_ref, lk1_ref, lq2_ref, lk2_ref, wsub_ref, q_ref, kn_ref, vn_ref,
                    *rest, pages_per_step, n_heads, dk, page, lam_init):
    del pt_ref
    k_refs = rest[:pages_per_step]
    v_refs = rest[pages_per_step:2 * pages_per_step]
    o_ref, oa_ref, qm_ref, base_ref, m_ref, l_ref, acc_ref = rest[2 * pages_per_step:]
    g = pl.program_id(1)
    n_rows = 2 * n_heads
    n_cols = page * n_heads
    head_bits = int(math.log2(n_heads))
    past_len = pl.num_programs(1) * pages_per_step * page

    rowi = lax.broadcasted_iota(jnp.int32, (n_rows, 1), 0)
    head = rowi & (n_heads - 1)
    coef = jnp.exp2(-8.0 * (head + 1).astype(F32) / n_heads) * LOG2E

    @pl.when(g == 0)
    def _():
        o_ref[...] = jnp.zeros(o_ref.shape, F32)
        q = q_ref[0].astype(F32)
        lane = lax.broadcasted_iota(jnp.int32, q.shape, 1)
        qm = jnp.concatenate([jnp.where(lane < dk, q, 0.0), jnp.where(lane >= dk, q, 0.0)], axis=0)
        qm_ref[...] = qm.astype(BF16)
        col = lax.broadcasted_iota(jnp.int32, (n_rows, n_cols), 1)
        own = (col & (n_heads - 1)) == head
        tok = lax.shift_right_logical(col, head_bits)
        dist = (past_len - tok).astype(F32)
        base_ref[...] = jnp.where(own, -coef * dist, -jnp.inf)
        m_ref[...] = jnp.full(m_ref.shape, -jnp.inf, F32)
        l_ref[...] = jnp.zeros(l_ref.shape, F32)
        acc_ref[...] = jnp.zeros(acc_ref.shape, F32)

    qm = qm_ref[...]
    scores, offs, top = [], [], None
    for r in range(pages_per_step):
        off = coef * ((g * pages_per_step + r) * page).astype(F32)
        kp = k_refs[r][...].astype(BF16)
        s = lax.dot_general(qm, kp, (((1,), (1,)), ((), ())), preferred_element_type=F32) + base_ref[...]
        cand = jnp.max(s, axis=-1, keepdims=True) + off
        top = cand if top is None else jnp.maximum(top, cand)
        scores.append(s)
        offs.append(off)
    m_old = m_ref[...]
    m_new = jnp.maximum(m_old, top)
    alpha = jnp.exp2(m_old - m_new)
    l_new = alpha * l_ref[...]
    acc_new = alpha * acc_ref[...]
    for r in range(pages_per_step):
        p = jnp.exp2(scores[r] - (m_new - offs[r]))
        l_new = l_new + jnp.sum(p, axis=-1, keepdims=True)
        acc_new = acc_new + jnp.dot(p.astype(BF16), v_refs[r][...].astype(BF16), preferred_element_type=F32)
    l_ref[...] = l_new
    acc_ref[...] = acc_new
    m_ref[...] = m_new

    _mlp_accumulate(u_ref, wu_ref, wd_ref, o_ref)

    @pl.when(g == pl.num_programs(1) - 1)
    def _():
        kn = jnp.concatenate([kn_ref[0], kn_ref[0]], axis=0)
        vn = jnp.concatenate([vn_ref[0], vn_ref[0]], axis=0)
        s_new = jnp.sum(qm_ref[...].astype(F32) * kn, axis=-1, keepdims=True)
        m_old = m_ref[...]
        m_new = jnp.maximum(m_old, s_new)
        p_new = jnp.exp2(s_new - m_new)
        alpha = jnp.exp2(m_old - m_new)
        o_all = (alpha * acc_ref[...] + p_new * vn) / (alpha * l_ref[...] + p_new)
        lam = _lambda_value(lq1_ref, lk1_ref, lq2_ref, lk2_ref, lam_init)
        o = o_all[0:n_heads] - lam * o_all[n_heads:n_rows]
        oa_ref[0] = (_rms(o, SUBLN_EPS) * wsub_ref[...] * (1.0 - lam_init)).astype(oa_ref.dtype)


def _mlp_with_paged_attention(u, w_up, w_down, q_s, k_new, v_new, cache_k, cache_v, page_table, lams, w_subln, *,
                              layer, n_heads, dk, dv, lam_init, tf=512):
    m_rows, d = u.shape
    d_ff = w_up.shape[1]
    db, n_pages = page_table.shape
    depth, n_pool, page, _, _ = cache_k.shape
    assert m_rows % db == 0 and d_ff % tf == 0
    tm, n_groups = m_rows // db, d_ff // tf
    assert n_pages % n_groups == 0 and n_heads & (n_heads - 1) == 0
    pages_per_step = n_pages // n_groups
    rows = page * n_heads
    ck = cache_k.reshape(depth * n_pool * rows, 2 * dk)
    cv = cache_v.reshape(depth * n_pool * rows, dv)
    base = layer * n_pool

    def page_spec(r, width):
        return pl.BlockSpec((rows, width), lambda b, g, pt: (base + pt[b * n_pages + g * pages_per_step + r], 0))

    lam_spec = pl.BlockSpec((1, dk), lambda b, g, pt: (0, 0))
    tok_spec = pl.BlockSpec((1, n_heads, dv), lambda b, g, pt: (b, 0, 0))
    n_rows = 2 * n_heads
    grid_spec = pltpu.PrefetchScalarGridSpec(
        num_scalar_prefetch=1,
        grid=(db, n_groups),
        in_specs=[pl.BlockSpec((tm, d), lambda b, g, pt: (b, 0)),
                  pl.BlockSpec((d, tf), lambda b, g, pt: (0, g)),
                  pl.BlockSpec((tf, d), lambda b, g, pt: (g, 0)),
                  lam_spec, lam_spec, lam_spec, lam_spec,
                  pl.BlockSpec((1, dv), lambda b, g, pt: (0, 0)),
                  tok_spec, tok_spec, tok_spec]
                 + [page_spec(r, 2 * dk) for r in range(pages_per_step)]
                 + [page_spec(r, dv) for r in range(pages_per_step)],
        out_specs=[pl.BlockSpec((tm, d), lambda b, g, pt: (b, 0)), tok_spec],
        scratch_shapes=[
            pltpu.VMEM((n_rows, 2 * dk), BF16),
            pltpu.VMEM((n_rows, rows), F32),
            pltpu.VMEM((n_rows, 1), F32),
            pltpu.VMEM((n_rows, 1), F32),
            pltpu.VMEM((n_rows, dv), F32),
        ],
    )
    heads = lambda a: a.reshape(db, n_heads, dv)
    mlp_out, o_attn = pl.pallas_call(
        functools.partial(_mlp_paged_body, pages_per_step=pages_per_step, n_heads=n_heads, dk=dk, page=page,
                          lam_init=lam_init),
        grid_spec=grid_spec,
        out_shape=[jax.ShapeDtypeStruct((m_rows, d), F32), jax.ShapeDtypeStruct((db, n_heads, dv), BF16)],
        compiler_params=_params("arbitrary", "arbitrary"),
        name="relu2_mlp_with_sample_paged_attention",
    )(page_table.reshape(-1), u, w_up, w_down, *lams, w_subln, heads(q_s), heads(k_new), heads(v_new),
      *([ck] * pages_per_step), *([cv] * pages_per_step))
    return mlp_out, o_attn.reshape(db, n_heads * dv)


def _to_column(row_vec, eye):
    n = eye.shape[0]
    return jnp.sum(jnp.where(eye, jnp.broadcast_to(row_vec, (n, n)), 0.0), axis=-1, keepdims=True)


def _to_row(col_vec, eye):
    n = eye.shape[0]
    return jnp.sum(jnp.where(eye, jnp.broadcast_to(col_vec, (n, n)), 0.0), axis=0, keepdims=True)


def _ssm_step_body(xbc_ref, z_ref, dt_ref, sc_ref, st_ref, cw_ref, cb_ref, dtb_ref, alog_ref, dskip_ref, wn_ref,
                   e_ref, o_ref, st_out_ref, *, conv_w, d_inner, d_state, n_groups, heads_per_group, headdim):
    gw = heads_per_group * headdim
    conv = cb_ref[...] + xbc_ref[0] * cw_ref[conv_w - 1:conv_w, :]
    sc = sc_ref[0]
    for j in range(conv_w - 1):
        conv = conv + sc[j:j + 1, :] * cw_ref[j:j + 1, :]
    act = _silu(conv)
    xs = act[:, :d_inner]
    gn = n_groups * d_state

    dtv = _softplus(dt_ref[0] + dtb_ref[...])
    da = dtv * (-jnp.exp(alog_ref[...]))
    expand = e_ref[...]
    dt_x = jnp.dot(dtv, expand, precision=HIGHEST, preferred_element_type=F32)
    decay_x = jnp.exp(jnp.dot(da, expand, precision=HIGHEST, preferred_element_type=F32))
    xd = xs * dt_x

    eye = (lax.broadcasted_iota(jnp.int32, (LANES, LANES), 0)
           == lax.broadcasted_iota(jnp.int32, (LANES, LANES), 1))
    y_tiles = []
    for t in range(d_inner // LANES):
        g = (t * LANES) // gw
        rows = slice(t * LANES, (t + 1) * LANES)
        b_row = act[:, d_inner + g * d_state:d_inner + (g + 1) * d_state]
        c_row = act[:, d_inner + gn + g * d_state:d_inner + gn + (g + 1) * d_state]
        st = st_ref[0, rows, :] * _to_column(decay_x[:, rows], eye) + _to_column(xd[:, rows], eye) * b_row
        st_out_ref[0, rows, :] = st
        y_tiles.append(_to_row(jnp.sum(st * c_row, axis=-1, keepdims=True), eye))
    y = jnp.concatenate(y_tiles, axis=-1) + dskip_ref[...] * xs
    gated = y * _silu(z_ref[0])
    wn = wn_ref[...]
    outs = []
    for g in range(n_groups):
        gs = slice(g * gw, (g + 1) * gw)
        outs.append(_rms(gated[:, gs], NORM_EPS) * wn[:, gs])
    o_ref[0] = jnp.concatenate(outs, axis=-1).astype(o_ref.dtype)


def _sample_ssm(xbc, z, dt, state_conv, state_ssm, conv_w, conv_b, dt_bias_p, a_log_p, dskip_x, w_norm, expand, *,
                d_state, n_groups, n_heads, headdim):
    db, conv_dim = xbc.shape
    d_inner = n_heads * headdim
    cw = conv_w.shape[0]
    per_b = lambda *tail: pl.BlockSpec((1,) + tail, lambda b: (b,) + (0,) * len(tail))
    const = lambda shape: pl.BlockSpec(shape, lambda b: (0, 0))
    o, st = pl.pallas_call(
        functools.partial(_ssm_step_body, conv_w=cw, d_inner=d_inner, d_state=d_state, n_groups=n_groups,
                          heads_per_group=n_heads // n_groups, headdim=headdim),
        grid=(db,),
        in_specs=[
            per_b(1, conv_dim), per_b(1, d_inner), per_b(1, LANES), per_b(cw - 1, conv_dim),
            per_b(d_inner, d_state),
            const((cw, conv_dim)), const((1, conv_dim)), const((1, LANES)), const((1, LANES)),
            const((1, d_inner)), const((1, d_inner)), const((LANES, d_inner)),
        ],
        out_specs=[per_b(1, d_inner), per_b(d_inner, d_state)],
        out_shape=[jax.ShapeDtypeStruct((db, 1, d_inner), BF16),
                   jax.ShapeDtypeStruct((db, d_inner, d_state), F32)],
        compiler_params=_params("arbitrary"),
        name="sample_ssm_step",
    )(xbc.reshape(db, 1, conv_dim), z.reshape(db, 1, d_inner), dt.reshape(db, 1, LANES), state_conv,
      state_ssm.reshape(db, d_inner, d_state), conv_w, conv_b, dt_bias_p, a_log_p, dskip_x, w_norm, expand)
    return o.reshape(db, d_inner), st


def _pad_lanes(v):
    return jnp.pad(v.astype(F32), (0, LANES - v.shape[0])).reshape(1, LANES)


def kernel(x_prompt, x_sample, p_prompt, p_sample, cache_k, cache_v, page_table, state_ssm, state_conv, g_mix, w_in, lambda_q1, lambda_k1, lambda_q2, lambda_k2, w_subln, conv_w, conv_b, dt_bias, a_log, d_skip, w_ssm_norm, w_out, g_ffn, w_up, w_down, w_ple, w_ple_norm, w_ple_gate, g_final):
    batch, seq, d_model = x_prompt.shape
    db, dec_seq, _ = x_sample.shape
    assert dec_seq == 1, "the sample kernels handle one new token per sequence"
    depth = w_in.shape[0]
    n_heads_a = cache_k.shape[3]
    dv = cache_v.shape[4]
    dk = lambda_q1.shape[1]
    assert cache_k.shape[4] == 2 * dk == dv == LANES
    aw = n_heads_a * dv
    n_heads_s, headdim, d_state = state_ssm.shape[2:]
    d_inner = n_heads_s * headdim
    conv_dim = state_conv.shape[3]
    n_groups = (conv_dim - d_inner) // (2 * d_state)
    assert n_heads_s <= LANES and LANES % headdim == 0 and d_state == LANES
    n_main = 3 * aw + d_inner + conv_dim
    q_scale = float(dk) ** -0.5 * LOG2E
    expand = _head_expand(n_heads_s, headdim)

    hp = x_prompt.reshape(batch * seq, d_model)
    hs = x_sample.reshape(db, d_model)
    outs = {name: [] for name in ("kp", "vp", "sp", "cp", "ks", "vs", "ss", "cs")}
    for i in range(depth):
        lam_init = 0.8 - 0.6 * math.exp(-0.3 * i)
        last = i == depth - 1
        w_main = w_in[i].astype(BF16)
        w_dt = jnp.pad(w_in[i][:, n_main:].astype(BF16), ((0, 0), (0, LANES - n_heads_s)))
        w_ple_i = w_ple[i].astype(BF16)
        g_mix_i, g_ffn_i = g_mix[i].reshape(1, -1), g_ffn[i].reshape(1, -1)
        lams = tuple(v[i].reshape(1, dk) for v in (lambda_q1, lambda_k1, lambda_q2, lambda_k2))
        w_subln_i = w_subln[i].reshape(1, dv)
        conv_b_i = conv_b[i].reshape(1, conv_dim)
        dt_bias_p, a_log_p = _pad_lanes(dt_bias[i]), _pad_lanes(a_log[i])
        dskip_x = jnp.repeat(d_skip[i].astype(F32), headdim).reshape(1, d_inner)
        w_norm_i = w_ssm_norm[i].reshape(1, d_inner)
        w_ple_norm_i = w_ple_norm[i].reshape(1, d_model)
        g_final_r = g_final.reshape(1, d_model)
        proj = functools.partial(_inproj, g=g_mix_i, w_main=w_main, w_dt=w_dt, widths=(aw, d_inner, conv_dim),
                                 q_scale=q_scale)
        ssm_kw = dict(d_state=d_state, n_groups=n_groups, n_heads=n_heads_s, headdim=headdim)

        (q_bf, k32, v32, k_bf, v_bf, z, xbc, dt), (qs_bf, ks32, vs32, zs, xbcs, dts) = proj(hp, hs)
        o_attn, (w_out_i, w_up_i, w_down_i, w_gate_i) = _prompt_attention(
            q_bf, k_bf, v_bf, lams, w_subln_i, (w_out[i], w_up[i], w_down[i], w_ple_gate[i]), batch=batch, seq=seq,
            n_heads=n_heads_a, dk=dk, dv=dv, lam_init=lam_init)
        ple = functools.partial(_ple, w_ple=w_ple_i, w_ple_norm=w_ple_norm_i, w_gate=w_gate_i, g_final=g_final_r,
                                final_norm=last)
        o_ssm, st = _prompt_ssd(xbc, z, dt, conv_w[i], conv_b_i, dt_bias_p, a_log_p, dskip_x, w_norm_i, expand,
                                batch=batch, seq=seq, **ssm_kw)
        h1, u = _outproj(o_attn, o_ssm, hp, w_out_i, g_ffn_i)
        outs["kp"].append(k32.reshape(batch, seq, n_heads_a, 2 * dk))
        outs["vp"].append(v32.reshape(batch, seq, n_heads_a, dv))
        outs["sp"].append(st.reshape(batch, n_heads_s, headdim, d_state))
        cw = conv_w.shape[1]
        outs["cp"].append(xbc.reshape(batch, seq, conv_dim)[:, seq - (cw - 1):])

        mlp_out, os_attn = _mlp_with_paged_attention(u, w_up_i, w_down_i, qs_bf, ks32, vs32, cache_k, cache_v,
                                                     page_table, lams, w_subln_i, layer=i, n_heads=n_heads_a, dk=dk,
                                                     dv=dv, lam_init=lam_init)

        os_ssm, st = _sample_ssm(xbcs, zs, dts, state_conv[i], state_ssm[i], conv_w[i], conv_b_i, dt_bias_p, a_log_p,
                                 dskip_x, w_norm_i, expand, **ssm_kw)
        h1s, us = _outproj(os_attn, os_ssm, hs, w_out_i, g_ffn_i)
        hp, mlp_out_s = ple(h1, mlp_out, p_prompt[i].reshape(batch * seq, -1), small_mlp=(us, w_up_i, w_down_i))
        hs = ple(h1s, mlp_out_s, p_sample[i].reshape(db, -1))
        outs["ks"].append(ks32.reshape(db, 1, n_heads_a, 2 * dk))
        outs["vs"].append(vs32.reshape(db, 1, n_heads_a, dv))
        outs["ss"].append(st.reshape(db, n_heads_s, headdim, d_state))
        outs["cs"].append(jnp.concatenate([state_conv[i][:, 1:], xbcs[:, None, :]], axis=1))

    stack = lambda name: jnp.stack(outs[name])
    return (hp.reshape(batch, seq, d_model), hs.reshape(db, 1, d_model), stack("kp"), stack("vp"), stack("sp"),
            stack("cp"), stack("ks"), stack("vs"), stack("ss"), stack("cs"))
```
